```python
import math
import jax, jax.numpy as jnp
from jax import lax
import numpy as np

D_MODEL = 1024
BATCH = 2
SEQ = 16384
DEPTH = 1

HEAD_DIM = 64
SWA_Q_HEADS = 8
SWA_KV_HEADS = 2
SWA_GROUP = SWA_Q_HEADS // SWA_KV_HEADS
WINDOW = 128
BLOCK = 128
DIFF_HEADS = 4
DIFF_V_DIM = 2 * HEAD_DIM
N_HEADS_TOTAL = SWA_Q_HEADS + DIFF_HEADS
N_BUCKETS = 32
MAX_DISTANCE = 128
D_FF = 2816
CONV_WIDTH = 3
N_BRANCH = 2
EPS = 1e-6

SWA_Q_W = SWA_Q_HEADS * HEAD_DIM
SWA_KV_W = SWA_KV_HEADS * HEAD_DIM
DIFF_QK_W = DIFF_HEADS * 2 * HEAD_DIM
DIFF_V_W = DIFF_HEADS * DIFF_V_DIM
IN_SPLITS = (SWA_Q_W, SWA_KV_W, SWA_KV_W, DIFF_QK_W, DIFF_QK_W, DIFF_V_W, N_BRANCH * D_MODEL)
IN_WIDTH = sum(IN_SPLITS)

kernel_name = "hybrid_swa_sink_diffattn_convffn"


def rms_norm(x, g):
    xf = x.astype(jnp.float32)
    y = xf * lax.rsqrt(jnp.mean(xf * xf, axis=-1, keepdims=True) + EPS)
    return (y * g.astype(jnp.float32)).astype(x.dtype)


def t5_bucket(rel):
    n = jnp.maximum(rel, 0)
    max_exact = N_BUCKETS // 2
    nf = jnp.maximum(n, 1).astype(jnp.float32)
    large = max_exact + (jnp.log(nf / max_exact) / math.log(MAX_DISTANCE / max_exact)
                         * (N_BUCKETS - max_exact)).astype(jnp.int32)
    large = jnp.minimum(large, N_BUCKETS - 1)
    return jnp.where(n < max_exact, n, large)


def swa_sink_attention(q, k, v, sinks, bias_tab):
    B, S = q.shape[0], q.shape[1]
    nb = S // BLOCK
    qb = q.reshape(B, nb, BLOCK, SWA_KV_HEADS, SWA_GROUP, HEAD_DIM)
    kb = k.reshape(B, nb, BLOCK, SWA_KV_HEADS, HEAD_DIM)
    vb = v.reshape(B, nb, BLOCK, SWA_KV_HEADS, HEAD_DIM)
    pad = ((0, 0), (1, 0), (0, 0), (0, 0), (0, 0))
    kk = jnp.concatenate([jnp.pad(kb, pad)[:, :-1], kb], axis=2)
    vv = jnp.concatenate([jnp.pad(vb, pad)[:, :-1], vb], axis=2)
    s = jnp.einsum('bnqhgd,bnkhd->bnhgqk', qb, kk).astype(jnp.float32) * (HEAD_DIM ** -0.5)
    qi = jnp.arange(BLOCK)[:, None]
    kj = jnp.arange(2 * BLOCK)[None, :]
    rel = BLOCK + qi - kj
    band = (rel >= 0) & (rel < WINDOW)
    valid = band[None] & ((jnp.arange(nb)[:, None, None] > 0) | (kj[None] >= BLOCK))
    bias = jnp.take(bias_tab.astype(jnp.float32), t5_bucket(rel), axis=0)
    bias = bias.transpose(2, 0, 1).reshape(SWA_KV_HEADS, SWA_GROUP, BLOCK, 2 * BLOCK)
    s = jnp.where(valid[None, :, None, None], s + bias[None, None], -jnp.inf)
    sink = jnp.broadcast_to(sinks.astype(jnp.float32).reshape(1, 1, SWA_KV_HEADS, SWA_GROUP, 1, 1),
                            s.shape[:-1] + (1,))
    p = jax.nn.softmax(jnp.concatenate([s, sink], axis=-1), axis=-1)[..., :-1]
    o = jnp.einsum('bnhgqk,bnkhd->bnqhgd', p.astype(v.dtype), vv)
    return o.reshape(B, S, SWA_Q_HEADS * HEAD_DIM)


def diff_attention(q, k, v, lam, bias_tab):
    B, S = q.shape[0], q.shape[1]
    nb = S // BLOCK
    qblocks = jnp.moveaxis(q.reshape(B, nb, BLOCK, DIFF_HEADS, 2, HEAD_DIM), 1, 0)
    k_pos = jnp.arange(S)
    tab = bias_tab.astype(jnp.float32)

    def one_block(args):
        qi, n = args
        s = jnp.einsum('bqhcd,bkhcd->bhcqk', qi, k).astype(jnp.float32) * (HEAD_DIM ** -0.5)
        rel = (n * BLOCK + jnp.arange(BLOCK))[:, None] - k_pos[None, :]
        bias = jnp.take(tab, t5_bucket(rel), axis=0).transpose(2, 0, 1)
        s = jnp.where((rel >= 0)[None, None, None], s + bias[None, :, None], -jnp.inf)
        p = jax.nn.softmax(s, axis=-1)
        a = (p[:, :, 0] - lam * p[:, :, 1]).astype(v.dtype)
        return jnp.einsum('bhqk,bkhe->bqhe', a, v)

    out = lax.map(one_block, (qblocks, jnp.arange(nb)))
    return jnp.moveaxis(out, 0, 1).reshape(B, S, DIFF_HEADS, DIFF_V_DIM)


def causal_dwconv(u, w, b):
    C = u.shape[-1]
    y = lax.conv_general_dilated(u, w.reshape(CONV_WIDTH, 1, C).astype(u.dtype),
                                 window_strides=(1,), padding=[(CONV_WIDTH - 1, 0)],
                                 dimension_numbers=('NWC', 'WIO', 'NWC'),
                                 feature_group_count=C)
    return y + b.astype(u.dtype)


def split_cols(z, sizes):
    outs, off = [], 0
    for sz in sizes:
        outs.append(z[..., off:off + sz])
        off += sz
    return outs


def setup_inputs(seed: int = 0) -> dict:
    key = jax.random.key(seed)
    ks = jax.random.split(key, 24)
    f32 = jnp.float32

    def nrm(k, shape, scale):
        return jax.random.normal(k, shape, f32) * scale

    return {
        "x": nrm(ks[0], (BATCH, SEQ, D_MODEL), 1.0),
        "rel_bias": nrm(ks[1], (N_BUCKETS, N_HEADS_TOTAL), 0.5),
        "g_mix": 1.0 + nrm(ks[2], (DEPTH, D_MODEL), 0.02),
        "w_in": nrm(ks[3], (DEPTH, D_MODEL, IN_WIDTH), D_MODEL ** -0.5),
        "qn_a": 1.0 + nrm(ks[4], (DEPTH, HEAD_DIM), 0.02),
        "kn_a": 1.0 + nrm(ks[5], (DEPTH, HEAD_DIM), 0.02),
        "sinks": nrm(ks[6], (DEPTH, SWA_Q_HEADS), 0.5),
        "qn_b": 1.0 + nrm(ks[7], (DEPTH, HEAD_DIM), 0.02),
        "kn_b": 1.0 + nrm(ks[8], (DEPTH, HEAD_DIM), 0.02),
        "lam_q1": nrm(ks[9], (DEPTH, HEAD_DIM), 0.1),
        "lam_k1": nrm(ks[10], (DEPTH, HEAD_DIM), 0.1),
        "lam_q2": nrm(ks[11], (DEPTH, HEAD_DIM), 0.1),
        "lam_k2": nrm(ks[12], (DEPTH, HEAD_DIM), 0.1),
        "subln_b": 1.0 + nrm(ks[13], (DEPTH, DIFF_V_DIM), 0.02),
        "w_br_a": nrm(ks[14], (DEPTH, SWA_Q_W, D_MODEL), SWA_Q_W ** -0.5),
        "w_br_b": nrm(ks[15], (DEPTH, DIFF_V_W, D_MODEL), DIFF_V_W ** -0.5),
        "w_o": nrm(ks[16], (DEPTH, D_MODEL, D_MODEL), D_MODEL ** -0.5),
        "g_ffn": 1.0 + nrm(ks[17], (DEPTH, D_MODEL), 0.02),
        "w_up": nrm(ks[18], (DEPTH, D_MODEL, 2 * D_FF), D_MODEL ** -0.5),
        "conv_w": nrm(ks[19], (DEPTH, CONV_WIDTH, 2 * D_FF), CONV_WIDTH ** -0.5),
        "conv_b": nrm(ks[20], (DEPTH, 2 * D_FF), 0.02),
        "w_down": nrm(ks[21], (DEPTH, D_FF, D_MODEL), D_FF ** -0.5),
    }


def reference(x, rel_bias, g_mix, w_in, qn_a, kn_a, sinks, qn_b, kn_b, lam_q1, lam_k1,
              lam_q2, lam_k2, subln_b, w_br_a, w_br_b, w_o, g_ffn, w_up, conv_w, conv_b, w_down):
    B, S = x.shape[0], x.shape[1]
    bias_a = rel_bias[:, :SWA_Q_HEADS]
    bias_b = rel_bias[:, SWA_Q_HEADS:]
    for l in range(DEPTH):
        lam_init = 0.8 - 0.6 * math.exp(-0.3 * l)
        h = rms_norm(x, g_mix[l])
        z = h @ w_in[l]
        qa, ka, va, qb, kb, vb, gl = split_cols(z, IN_SPLITS)
        qa = rms_norm(qa.reshape(B, S, SWA_Q_HEADS, HEAD_DIM), qn_a[l])
        ka = rms_norm(ka.reshape(B, S, SWA_KV_HEADS, HEAD_DIM), kn_a[l])
        va = va.reshape(B, S, SWA_KV_HEADS, HEAD_DIM)
        ya = swa_sink_attention(qa, ka, va, sinks[l], bias_a)

        qb = rms_norm(qb.reshape(B, S, DIFF_HEADS, 2, HEAD_DIM), qn_b[l])
        kb = rms_norm(kb.reshape(B, S, DIFF_HEADS, 2, HEAD_DIM), kn_b[l])
        vb = vb.reshape(B, S, DIFF_HEADS, DIFF_V_DIM)
        lam = (jnp.exp(jnp.sum(lam_q1[l].astype(jnp.float32) * lam_k1[l].astype(jnp.float32)))
               - jnp.exp(jnp.sum(lam_q2[l].astype(jnp.float32) * lam_k2[l].astype(jnp.float32)))
               + lam_init)
        ob = diff_attention(qb, kb, vb, lam, bias_b)
        yb = (rms_norm(ob, subln_b[l]) * (1.0 - lam_init)).reshape(B, S, DIFF_V_W)

        gates = jax.nn.sigmoid(gl.reshape(B, S, N_BRANCH, D_MODEL).astype(jnp.float32)).astype(x.dtype)
        mixed = gates[:, :, 0] * (ya @ w_br_a[l]) + gates[:, :, 1] * (yb @ w_br_b[l])
        x = x + mixed @ w_o[l]
        h = rms_norm(x, g_ffn[l])
        u = causal_dwconv(h @ w_up[l], conv_w[l], conv_b[l])
        ug, uv = u[..., :D_FF], u[..., D_FF:]
        x = x + (jax.nn.silu(ug) * uv) @ w_down[l]
    return x
```

```python
import functools
import math

import numpy as np
import jax
import jax.numpy as jnp
from jax import lax
from jax.experimental import pallas as pl
from jax.experimental.pallas import tpu as pltpu

D_MODEL = 1024
HEAD_DIM = 64
SWA_Q_HEADS = 8
SWA_KV_HEADS = 2
SWA_GROUP = SWA_Q_HEADS // SWA_KV_HEADS
WINDOW = 128
BLOCK = 128
DIFF_HEADS = 4
DIFF_V_DIM = 2 * HEAD_DIM
N_BUCKETS = 32
MAX_DISTANCE = 128
D_FF = 2816
CONV_WIDTH = 3
EPS = 1e-6

SWA_Q_W = SWA_Q_HEADS * HEAD_DIM
SWA_KV_W = SWA_KV_HEADS * HEAD_DIM
DIFF_QK_W = DIFF_HEADS * 2 * HEAD_DIM
DIFF_V_W = DIFF_HEADS * DIFF_V_DIM
GATE_W = 2 * D_MODEL
IN_WIDTH = SWA_Q_W + 2 * SWA_KV_W + 2 * DIFF_QK_W + DIFF_V_W + GATE_W

LANES = 128
SUBLANES = 8
LOG2E = math.log2(math.e)
NEG_INF = float("-inf")

ROW_TILE = 512
ATTN_TILE = 512
FF_CHUNK = 256
VMEM_LIMIT = 56 * 1024 * 1024

F32 = jnp.float32
BF16 = jnp.bfloat16


def _t5_bucket_np(rel):
    n = np.maximum(rel, 0)
    max_exact = N_BUCKETS // 2
    nf = np.maximum(n, 1).astype(np.float64)
    large = max_exact + (np.log(nf / max_exact) / math.log(MAX_DISTANCE / max_exact)
                         * (N_BUCKETS - max_exact)).astype(np.int32)
    large = np.minimum(large, N_BUCKETS - 1)
    return np.where(n < max_exact, n, large).astype(np.int32)


def _dot(a, b):
    return jnp.dot(a, b, preferred_element_type=F32)


def _dot_nt(a, b):
    return lax.dot_general(a, b, (((1,), (1,)), ((), ())), preferred_element_type=F32)


def _half_lane_mask(shape):
    return lax.broadcasted_iota(jnp.int32, shape, len(shape) - 1) < HEAD_DIM


def _bias_table_kernel(rb_ref, idx_ref, out_ref, *, col0, shift_far):
    h = pl.program_id(0) + col0
    idx = idx_ref[...]
    shift = rb_ref[N_BUCKETS - 1, h] if shift_far else 0.0
    acc = jnp.full(idx.shape, NEG_INF, F32)
    for b in range(N_BUCKETS):
        acc = jnp.where(idx == b, (rb_ref[b, h] - shift) * LOG2E, acc)
    out_ref[...] = acc


def _bias_tables(rel_bias, idx, n_heads, col0, shift_far):
    r, c = idx.shape
    return pl.pallas_call(
        functools.partial(_bias_table_kernel, col0=col0, shift_far=shift_far),
        grid=(n_heads,),
        in_specs=[pl.BlockSpec(memory_space=pltpu.SMEM),
                  pl.BlockSpec((r, c), lambda h: (0, 0))],
        out_specs=pl.BlockSpec((None, r, c), lambda h: (h, 0, 0)),
        out_shape=jax.ShapeDtypeStruct((n_heads, r, c), F32),
        name="bias_tables",
    )(rel_bias, jnp.asarray(idx))


def _pair_rms_norm(z, gain):
    outs = []
    for j in range(z.shape[1] // LANES):
        s = z[:, j * LANES:(j + 1) * LANES]
        left = _half_lane_mask(s.shape)
        sq = s * s
        tot = jnp.sum(sq, axis=-1, keepdims=True)
        lo = jnp.sum(jnp.where(left, sq, 0.0), axis=-1, keepdims=True)
        ms = jnp.where(left, lo, tot - lo) * (1.0 / HEAD_DIM)
        outs.append(s * lax.rsqrt(ms + EPS))
    return jnp.concatenate(outs, axis=-1) * gain


def _dup_halves(z):
    left = _half_lane_mask(z.shape)
    rolled = pltpu.roll(z, HEAD_DIM, axis=1)
    return jnp.concatenate([jnp.where(left, z, rolled), jnp.where(left, rolled, z)], axis=-1)


def _in_proj_kernel(x_ref, g_ref, w_ref, gqa_ref, gka_ref, gqb_ref, gkb_ref,
                    qa_ref, kd_ref, vd_ref, qb_ref, kb_ref, vb_ref, gate_ref):
    x = x_ref[...]
    h = (x * lax.rsqrt(jnp.mean(x * x, axis=-1, keepdims=True) + EPS) * g_ref[...]).astype(BF16)

    off = 0

    def proj(width):
        nonlocal off
        z = _dot(h, w_ref[:, off:off + width])
        off += width
        return z

    qa_ref[...] = _pair_rms_norm(proj(SWA_Q_W), gqa_ref[...]).astype(BF16)
    kd_ref[...] = _dup_halves(_pair_rms_norm(proj(SWA_KV_W), gka_ref[...])).astype(BF16)
    vd_ref[...] = _dup_halves(proj(SWA_KV_W)).astype(BF16)
    qb_ref[...] = _pair_rms_norm(proj(DIFF_QK_W), gqb_ref[...]).astype(BF16)
    kb_ref[...] = _pair_rms_norm(proj(DIFF_QK_W), gkb_ref[...]).astype(BF16)
    vb_ref[...] = proj(DIFF_V_W).astype(BF16)
    gate_ref[...] = jax.nn.sigmoid(proj(GATE_W))


def _in_proj(x2, g_mix, w_in, gqa, gka, gqb, gkb):
    t = x2.shape[0]
    tm = ROW_TILE
    row = lambda w: pl.BlockSpec((tm, w), lambda i: (i, 0))
    full = lambda a: pl.BlockSpec(a.shape, lambda i: (0, 0))
    widths = (SWA_Q_W, 2 * SWA_KV_W, 2 * SWA_KV_W, DIFF_QK_W, DIFF_QK_W, DIFF_V_W)
    return pl.pallas_call(
        _in_proj_kernel,
        grid=(t // tm,),
        in_specs=[row(D_MODEL), full(g_mix), full(w_in), full(gqa), full(gka), full(gqb), full(gkb)],
        out_specs=[row(w) for w in widths] + [row(GATE_W)],
        out_shape=[jax.ShapeDtypeStruct((t, w), BF16) for w in widths]
                  + [jax.ShapeDtypeStruct((t, GATE_W), F32)],
        compiler_params=pltpu.CompilerParams(dimension_semantics=("parallel",),
                                             vmem_limit_bytes=VMEM_LIMIT),
        name="in_proj",
    )(x2, g_mix, w_in, gqa, gka, gqb, gkb)


def _swa_kernel(q_ref, kd_ref, vd_ref, tbl_ref, sink_ref, o_ref):
    i = pl.program_id(1)
    blocks = ATTN_TILE // BLOCK

    def group_block(n, g, kk, vv, bias):
        rows = slice(n * BLOCK, (n + 1) * BLOCK)
        stacked = []
        for j in range(2):
            qp = q_ref[rows, (2 * g + j) * LANES:(2 * g + j + 1) * LANES]
            left = _half_lane_mask(qp.shape)
            zero = jnp.zeros_like(qp)
            stacked += [jnp.where(left, qp, zero), jnp.where(left, zero, qp)]
        s = _dot_nt(jnp.concatenate(stacked, axis=0), kk) + bias
        sink = sink_ref[g]
        m = jnp.maximum(jnp.max(s, axis=-1, keepdims=True), sink)
        p = jnp.exp2(s - m)
        denom = jnp.sum(p, axis=-1, keepdims=True) + jnp.exp2(sink - m)
        o = _dot(p.astype(BF16), vv) / denom
        for j in range(2):
            a = o[(2 * j) * BLOCK:(2 * j + 1) * BLOCK]
            b = o[(2 * j + 1) * BLOCK:(2 * j + 2) * BLOCK]
            left = _half_lane_mask(a.shape)
            o_ref[rows, (2 * g + j) * LANES:(2 * g + j + 1) * LANES] = jnp.where(left, a, b).astype(BF16)

    def with_previous(n, start):
        for g in range(SWA_KV_HEADS):
            cols = slice(g * LANES, (g + 1) * LANES)
            group_block(n, g, kd_ref[pl.ds(start, 2 * BLOCK), cols], vd_ref[pl.ds(start, 2 * BLOCK), cols],
                        tbl_ref[g])

    @pl.when(i == 0)
    def _():
        for g in range(SWA_KV_HEADS):
            cols = slice(g * LANES, (g + 1) * LANES)
            group_block(0, g, kd_ref[0:BLOCK, cols], vd_ref[0:BLOCK, cols], tbl_ref[g, :, BLOCK:])

    @pl.when(i > 0)
    def _():
        with_previous(0, pl.multiple_of(i * ATTN_TILE - BLOCK, BLOCK))

    for n in range(1, blocks):
        with_previous(n, pl.multiple_of(i * ATTN_TILE + (n - 1) * BLOCK, BLOCK))


def _swa_attention(qa, kd, vd, tbl, sink):
    b, s, _ = qa.shape
    tq = ATTN_TILE
    return pl.pallas_call(
        _swa_kernel,
        grid=(b, s // tq),
        in_specs=[pl.BlockSpec((None, tq, SWA_Q_W), lambda bi, i: (bi, i, 0)),
                  pl.BlockSpec((None, s, 2 * SWA_KV_W), lambda bi, i: (bi, 0, 0)),
                  pl.BlockSpec((None, s, 2 * SWA_KV_W), lambda bi, i: (bi, 0, 0)),
                  pl.BlockSpec(tbl.shape, lambda bi, i: (0, 0, 0)),
                  pl.BlockSpec(sink.shape, lambda bi, i: (0, 0, 0))],
        out_specs=pl.BlockSpec((None, tq, SWA_Q_W), lambda bi, i: (bi, i, 0)),
        out_shape=jax.ShapeDtypeStruct((b, s, SWA_Q_W), BF16),
        compiler_params=pltpu.CompilerParams(dimension_semantics=("parallel", "parallel"),
                                             vmem_limit_bytes=VMEM_LIMIT),
        name="swa_attention",
    )(qa, kd, vd, tbl, sink)


def _diff_kernel(lq1_ref, lk1_ref, lq2_ref, lk2_ref, q_ref, k_ref, v_ref, tbl_ref, gain_ref, o_ref,
                 m_sc, l_sc, acc_sc, *, lam_init):
    i = pl.program_id(2)
    tq = ATTN_TILE
    q = q_ref[...]
    left = _half_lane_mask(q.shape)
    zero = jnp.zeros_like(q)
    qs = (jnp.where(left, q, zero), jnp.where(left, zero, q))

    def first_step(kc, vc, bias):
        for c in range(2):
            s = _dot_nt(qs[c], kc) + bias
            m = jnp.max(s, axis=-1, keepdims=True)
            p = jnp.exp2(s - m)
            m_sc[c] = m
            l_sc[c] = jnp.sum(p, axis=-1, keepdims=True)
            acc_sc[c] = _dot(p.astype(BF16), vc)

    @pl.when(i == 0)
    def _():
        first_step(k_ref[0:tq, :], v_ref[0:tq, :], tbl_ref[:, tq:])

    @pl.when(i > 0)
    def _():
        start = pl.multiple_of((i - 1) * tq, tq)
        first_step(k_ref[pl.ds(start, 2 * tq), :], v_ref[pl.ds(start, 2 * tq), :], tbl_ref[...])

    def far_step(j, carry):
        start = pl.multiple_of(j * tq, tq)
        kc = k_ref[pl.ds(start, tq), :]
        vc = v_ref[pl.ds(start, tq), :]
        for c in range(2):
            s = _dot_nt(qs[c], kc)
            m_prev = m_sc[c]
            m_new = jnp.maximum(m_prev, jnp.max(s, axis=-1, keepdims=True))
            alpha = jnp.exp2(m_prev - m_new)
            p = jnp.exp2(s - m_new)
            m_sc[c] = m_new
            l_sc[c] = alpha * l_sc[c] + jnp.sum(p, axis=-1, keepdims=True)
            acc_sc[c] = alpha * acc_sc[c] + _dot(p.astype(BF16), vc)
        return carry

    lax.fori_loop(0, i - 1, far_step, 0)

    lam = (jnp.exp(jnp.sum(lq1_ref[...] * lk1_ref[...], axis=-1, keepdims=True))
           - jnp.exp(jnp.sum(lq2_ref[...] * lk2_ref[...], axis=-1, keepdims=True)) + lam_init)
    o = acc_sc[0] / l_sc[0] - lam * (acc_sc[1] / l_sc[1])
    y = o * lax.rsqrt(jnp.mean(o * o, axis=-1, keepdims=True) + EPS)
    o_ref[...] = (y * gain_ref[...] * (1.0 - lam_init)).astype(BF16)


def _diff_attention(lam_vecs, qb, kb, vb, tbl, gain, lam_init):
    b, s, _ = qb.shape
    tq = ATTN_TILE
    vec = pl.BlockSpec((1, HEAD_DIM), lambda bi, h, i: (0, 0))
    return pl.pallas_call(
        functools.partial(_diff_kernel, lam_init=lam_init),
        grid=(b, DIFF_HEADS, s // tq),
        in_specs=[vec, vec, vec, vec,
                  pl.BlockSpec((None, tq, LANES), lambda bi, h, i: (bi, i, h)),
                  pl.BlockSpec((None, s, LANES), lambda bi, h, i: (bi, 0, h)),
                  pl.BlockSpec((None, s, LANES), lambda bi, h, i: (bi, 0, h)),
                  pl.BlockSpec((None, tq, 2 * tq), lambda bi, h, i: (h, 0, 0)),
                  pl.BlockSpec((1, DIFF_V_DIM), lambda bi, h, i: (0, 0))],
        out_specs=pl.BlockSpec((None, tq, LANES), lambda bi, h, i: (bi, i, h)),
        out_shape=jax.ShapeDtypeStruct((b, s, DIFF_V_W), BF16),
        scratch_shapes=[pltpu.VMEM((2, tq, 1), F32), pltpu.VMEM((2, tq, 1), F32),
                        pltpu.VMEM((2, tq, DIFF_V_DIM), F32)],
        compiler_params=pltpu.CompilerParams(dimension_semantics=("parallel", "parallel", "arbitrary"),
                                             vmem_limit_bytes=VMEM_LIMIT),
        name="diff_attention",
    )(*lam_vecs, qb, kb, vb, tbl, gain)


def _merge_kernel(x_ref, ya_ref, yb_ref, gate_ref, wa_ref, wb_ref, wo_ref, o_ref):
    ga = gate_ref[:, :D_MODEL]
    gb = gate_ref[:, D_MODEL:]
    mixed = ga * _dot(ya_ref[...], wa_ref[...]) + gb * _dot(yb_ref[...], wb_ref[...])
    o_ref[...] = x_ref[...] + _dot(mixed.astype(BF16), wo_ref[...])


def _merge(x2, ya, yb, gates, wa, wb, wo):
    t = x2.shape[0]
    tm = ROW_TILE
    row = lambda w: pl.BlockSpec((tm, w), lambda i: (i, 0))
    full = lambda a: pl.BlockSpec(a.shape, lambda i: (0, 0))
    return pl.pallas_call(
        _merge_kernel,
        grid=(t // tm,),
        in_specs=[row(D_MODEL), row(SWA_Q_W), row(DIFF_V_W), row(GATE_W), full(wa), full(wb), full(wo)],
        out_specs=row(D_MODEL),
        out_shape=jax.ShapeDtypeStruct((t, D_MODEL), F32),
        compiler_params=pltpu.CompilerParams(dimension_semantics=("parallel",),
                                             vmem_limit_bytes=VMEM_LIMIT),
        name="merge",
    )(x2, ya, yb, gates, wa, wb, wo)


def _conv_mlp_kernel(x_ref, halo_ref, g_ref, wup_ref, cw_ref, cb_ref, wdn_ref, o_ref, ug_sc, uv_sc, acc_sc):
    tm = ROW_TILE
    first = pl.program_id(1) == 0

    def normed(x):
        return (x * lax.rsqrt(jnp.mean(x * x, axis=-1, keepdims=True) + EPS) * g_ref[...]).astype(BF16)

    x = x_ref[...]
    h = normed(x)
    h_halo = normed(halo_ref[...])
    keep = jnp.where(first, 0.0, 1.0)

    def conv(u_sc, cols):
        w = cw_ref[:, cols]
        y = (u_sc[SUBLANES - 2:SUBLANES - 2 + tm, :] * w[0:1]
             + u_sc[SUBLANES - 1:SUBLANES - 1 + tm, :] * w[1:2]
             + u_sc[SUBLANES:SUBLANES + tm, :] * w[2:3])
        return y + cb_ref[:, cols]

    for c in range(D_FF // FF_CHUNK):
        gcols = slice(c * FF_CHUNK, (c + 1) * FF_CHUNK)
        vcols = slice(D_FF + c * FF_CHUNK, D_FF + (c + 1) * FF_CHUNK)
        for u_sc, cols in ((ug_sc, gcols), (uv_sc, vcols)):
            u_sc[0:SUBLANES, :] = _dot(h_halo, wup_ref[:, cols]) * keep
            u_sc[SUBLANES:, :] = _dot(h, wup_ref[:, cols])
        ug = conv(ug_sc, gcols)
        uv = conv(uv_sc, vcols)
        act = (ug * jax.nn.sigmoid(ug) * uv).astype(BF16)
        contrib = _dot(act, wdn_ref[gcols, :])
        if c == 0:
            acc_sc[...] = contrib
        else:
            acc_sc[...] += contrib
    o_ref[...] = x + acc_sc[...]


def _conv_mlp(x3, g_ffn, w_up, conv_w, conv_b, w_down):
    b, s, _ = x3.shape
    tm = ROW_TILE
    halo_blocks = tm // SUBLANES
    full = lambda a: pl.BlockSpec(a.shape, lambda bi, i: (0,) * a.ndim)
    return pl.pallas_call(
        _conv_mlp_kernel,
        grid=(b, s // tm),
        in_specs=[pl.BlockSpec((None, tm, D_MODEL), lambda bi, i: (bi, i, 0)),
                  pl.BlockSpec((None, SUBLANES, D_MODEL),
                               lambda bi, i: (bi, jnp.maximum(i * halo_blocks - 1, 0), 0)),
                  full(g_ffn), full(w_up), full(conv_w), full(conv_b), full(w_down)],
        out_specs=pl.BlockSpec((None, tm, D_MODEL), lambda bi, i: (bi, i, 0)),
        out_shape=jax.ShapeDtypeStruct((b, s, D_MODEL), F32),
        scratch_shapes=[pltpu.VMEM((tm + SUBLANES, FF_CHUNK), F32),
                        pltpu.VMEM((tm + SUBLANES, FF_CHUNK), F32),
                        pltpu.VMEM((tm, D_MODEL), F32)],
        compiler_params=pltpu.CompilerParams(dimension_semantics=("parallel", "parallel"),
                                             vmem_limit_bytes=VMEM_LIMIT),
        name="conv_mlp",
    )(x3, x3, g_ffn, w_up, conv_w, conv_b, w_down)


def _swa_bucket_idx():
    rel = BLOCK + np.arange(BLOCK)[:, None] - np.arange(2 * BLOCK)[None, :]
    return np.where((rel >= 0) & (rel < WINDOW), _t5_bucket_np(rel), -1).astype(np.int32)


def _diff_bucket_idx():
    tq = ATTN_TILE
    rel = tq + np.arange(tq)[:, None] - np.arange(2 * tq)[None, :]
    return np.where(rel >= 0, _t5_bucket_np(rel), -1).astype(np.int32)


def _layer(x, l, rel_bias, swa_tbl, diff_tbl, g_mix, w_in, qn_a, kn_a, sinks, qn_b, kn_b, lam_q1, lam_k1,
           lam_q2, lam_k2, subln_b, w_br_a, w_br_b, w_o, g_ffn, w_up, conv_w, conv_b, w_down):
    b, s, _ = x.shape
    t = b * s
    lam_init = 0.8 - 0.6 * math.exp(-0.3 * l)
    q_scale = HEAD_DIM ** -0.5 * LOG2E
    tile = lambda v, reps, scale=1.0: (jnp.tile(v.astype(F32), reps) * scale).reshape(1, -1)

    x2 = x.reshape(t, D_MODEL)
    qa, kd, vd, qb, kb, vb, gates = _in_proj(
        x2, g_mix[l].reshape(1, -1), w_in[l].astype(BF16),
        tile(qn_a[l], SWA_Q_HEADS, q_scale), tile(kn_a[l], SWA_KV_HEADS),
        tile(qn_b[l], 2 * DIFF_HEADS, q_scale), tile(kn_b[l], 2 * DIFF_HEADS))

    sink = jnp.repeat(sinks[l].astype(F32) * LOG2E, BLOCK).reshape(SWA_KV_HEADS, SWA_GROUP * BLOCK, 1)
    ya = _swa_attention(qa.reshape(b, s, -1), kd.reshape(b, s, -1), vd.reshape(b, s, -1), swa_tbl, sink)

    lam_vecs = [v[l].astype(F32).reshape(1, HEAD_DIM) for v in (lam_q1, lam_k1, lam_q2, lam_k2)]
    yb = _diff_attention(lam_vecs, qb.reshape(b, s, -1), kb.reshape(b, s, -1), vb.reshape(b, s, -1),
                         diff_tbl, subln_b[l].astype(F32).reshape(1, -1), lam_init)

    x2 = _merge(x2, ya.reshape(t, -1), yb.reshape(t, -1), gates,
                w_br_a[l].astype(BF16), w_br_b[l].astype(BF16), w_o[l].astype(BF16))
    x3 = _conv_mlp(x2.reshape(b, s, D_MODEL), g_ffn[l].reshape(1, -1), w_up[l].astype(BF16),
                   conv_w[l].astype(F32), conv_b[l].astype(F32).reshape(1, -1), w_down[l].astype(BF16))
    return x3


def kernel(x, rel_bias, g_mix, w_in, qn_a, kn_a, sinks, qn_b, kn_b, lam_q1, lam_k1, lam_q2, lam_k2, subln_b,
           w_br_a, w_br_b, w_o, g_ffn, w_up, conv_w, conv_b, w_down):
    rb = rel_bias.astype(F32)
    swa_tbl = _bias_tables(rb, _swa_bucket_idx(), SWA_Q_HEADS, 0, False)
    swa_tbl = swa_tbl.reshape(SWA_KV_HEADS, SWA_GROUP * BLOCK, 2 * BLOCK)
    diff_tbl = _bias_tables(rb, _diff_bucket_idx(), DIFF_HEADS, SWA_Q_HEADS, True)
    for l in range(g_mix.shape[0]):
        x = _layer(x, l, rb, swa_tbl, diff_tbl, g_mix, w_in, qn_a, kn_a, sinks, qn_b, kn_b, lam_q1, lam_k1,
                   lam_q2, lam_k2, subln_b, w_br_a, w_br_b, w_o, g_ffn, w_up, conv_w, conv_b, w_down)
    return x
```

```python
import functools
import math

import numpy as np
import jax
import jax.numpy as jnp
from jax import lax
from jax.experimental import pallas as pl
from jax.experimental.pallas import tpu as pltpu

D_MODEL = 1024
HEAD_DIM = 64
SWA_Q_HEADS = 8
SWA_KV_HEADS = 2
SWA_GROUP = SWA_Q_HEADS // SWA_KV_HEADS
WINDOW = 128
BLOCK = 128
DIFF_HEADS = 4
DIFF_V_DIM = 2 * HEAD_DIM
N_BUCKETS = 32
MAX_DISTANCE = 128
D_FF = 2816
CONV_WIDTH = 3
EPS = 1e-6

SWA_Q_W = SWA_Q_HEADS * HEAD_DIM
SWA_KV_W = SWA_KV_HEADS * HEAD_DIM
DIFF_QK_W = DIFF_HEADS * 2 * HEAD_DIM
DIFF_V_W = DIFF_HEADS * DIFF_V_DIM
GATE_W = 2 * D_MODEL
IN_WIDTH = SWA_Q_W + 2 * SWA_KV_W + 2 * DIFF_QK_W + DIFF_V_W + GATE_W

LANES = 128
SUBLANES = 8
LOG2E = math.log2(math.e)
NEG_INF = float("-inf")

ROW_TILE = 512
ATTN_TILE = 512
FF_CHUNK = 256
VMEM_LIMIT = 56 * 1024 * 1024

F32 = jnp.float32
BF16 = jnp.bfloat16


def _t5_bucket_np(rel):
    n = np.maximum(rel, 0)
    max_exact = N_BUCKETS // 2
    nf = np.maximum(n, 1).astype(np.float64)
    large = max_exact + (np.log(nf / max_exact) / math.log(MAX_DISTANCE / max_exact)
                         * (N_BUCKETS - max_exact)).astype(np.int32)
    large = np.minimum(large, N_BUCKETS - 1)
    return np.where(n < max_exact, n, large).astype(np.int32)


def _dot(a, b):
    return jnp.dot(a, b, preferred_element_type=F32)


def _dot_nt(a, b):
    return lax.dot_general(a, b, (((1,), (1,)), ((), ())), preferred_element_type=F32)


def _half_lane_mask(shape):
    return lax.broadcasted_iota(jnp.int32, shape, len(shape) - 1) < HEAD_DIM


def _bias_table_kernel(rb_ref, idx_ref, out_ref, *, col0, shift_far):
    h = pl.program_id(0) + col0
    idx = idx_ref[...]
    shift = rb_ref[N_BUCKETS - 1, h] if shift_far else 0.0
    acc = jnp.full(idx.shape, NEG_INF, F32)
    for b in range(N_BUCKETS):
        acc = jnp.where(idx == b, (rb_ref[b, h] - shift) * LOG2E, acc)
    out_ref[...] = acc


def _bias_tables(rel_bias, idx, n_heads, col0, shift_far):
    r, c = idx.shape
    return pl.pallas_call(
        functools.partial(_bias_table_kernel, col0=col0, shift_far=shift_far),
        grid=(n_heads,),
        in_specs=[pl.BlockSpec(memory_space=pltpu.SMEM),
                  pl.BlockSpec((r, c), lambda h: (0, 0))],
        out_specs=pl.BlockSpec((None, r, c), lambda h: (h, 0, 0)),
        out_shape=jax.ShapeDtypeStruct((n_heads, r, c), F32),
        name="bias_tables",
    )(rel_bias, jnp.asarray(idx))


def _pair_rms_norm(z, gain):
    outs = []
    for j in range(z.shape[1] // LANES):
        s = z[:, j * LANES:(j + 1) * LANES]
        left = _half_lane_mask(s.shape)
        sq = s * s
        tot = jnp.sum(sq, axis=-1, keepdims=True)
        lo = jnp.sum(jnp.where(left, sq, 0.0), axis=-1, keepdims=True)
        ms = jnp.where(left, lo, tot - lo) * (1.0 / HEAD_DIM)
        outs.append(s * lax.rsqrt(ms + EPS))
    return jnp.concatenate(outs, axis=-1) * gain


def _dup_halves(z):
    left = _half_lane_mask(z.shape)
    rolled = pltpu.roll(z, HEAD_DIM, axis=1)
    return jnp.concatenate([jnp.where(left, z, rolled), jnp.where(left, rolled, z)], axis=-1)


def _in_proj_kernel(x_ref, g_ref, w_ref, gqa_ref, gka_ref, gqb_ref, gkb_ref,
                    qa_ref, kd_ref, vd_ref, qbt_ref, kb_ref, vbt_ref, gate_ref):
    x = x_ref[...]
    h = (x * lax.rsqrt(jnp.mean(x * x, axis=-1, keepdims=True) + EPS) * g_ref[...]).astype(BF16)

    off = 0

    def proj(width):
        nonlocal off
        z = _dot(h, w_ref[:, off:off + width])
        off += width
        return z

    qa_ref[...] = _pair_rms_norm(proj(SWA_Q_W), gqa_ref[...]).astype(BF16)
    kd_ref[...] = _dup_halves(_pair_rms_norm(proj(SWA_KV_W), gka_ref[...])).astype(BF16)
    vd_ref[...] = _dup_halves(proj(SWA_KV_W)).astype(BF16)
    qbt_ref[...] = _pair_rms_norm(proj(DIFF_QK_W), gqb_ref[...]).T.astype(BF16)
    kb_ref[...] = _pair_rms_norm(proj(DIFF_QK_W), gkb_ref[...]).astype(BF16)
    vbt_ref[...] = proj(DIFF_V_W).T.astype(BF16)
    gate_ref[...] = jax.nn.sigmoid(proj(GATE_W))


def _in_proj(x2, g_mix, w_in, gqa, gka, gqb, gkb):
    t = x2.shape[0]
    tm = ATTN_TILE
    row = lambda w: pl.BlockSpec((tm, w), lambda i: (i, 0))
    full = lambda a: pl.BlockSpec(a.shape, lambda i: (0, 0))
    tposed = lambda w: pl.BlockSpec((None, w, tm), lambda i: (i, 0, 0))
    bf = lambda w: jax.ShapeDtypeStruct((t, w), BF16)
    bft = lambda w: jax.ShapeDtypeStruct((t // tm, w, tm), BF16)
    return pl.pallas_call(
        _in_proj_kernel,
        grid=(t // tm,),
        in_specs=[row(D_MODEL), full(g_mix), full(w_in), full(gqa), full(gka), full(gqb), full(gkb)],
        out_specs=[row(SWA_Q_W), row(2 * SWA_KV_W), row(2 * SWA_KV_W), tposed(DIFF_QK_W), row(DIFF_QK_W),
                   tposed(DIFF_V_W), row(GATE_W)],
        out_shape=[bf(SWA_Q_W), bf(2 * SWA_KV_W), bf(2 * SWA_KV_W), bft(DIFF_QK_W), bf(DIFF_QK_W),
                   bft(DIFF_V_W), jax.ShapeDtypeStruct((t, GATE_W), F32)],
        compiler_params=pltpu.CompilerParams(dimension_semantics=("parallel",),
                                             vmem_limit_bytes=VMEM_LIMIT),
        name="in_proj",
    )(x2, g_mix, w_in, gqa, gka, gqb, gkb)


def _swa_kernel(q_ref, kd_ref, vd_ref, tbl_ref, sink_ref, o_ref):
    i = pl.program_id(1)
    blocks = ATTN_TILE // BLOCK

    def group_block(n, g, kk, vv, bias):
        rows = slice(n * BLOCK, (n + 1) * BLOCK)
        stacked = []
        for j in range(2):
            qp = q_ref[rows, (2 * g + j) * LANES:(2 * g + j + 1) * LANES]
            left = _half_lane_mask(qp.shape)
            zero = jnp.zeros_like(qp)
            stacked += [jnp.where(left, qp, zero), jnp.where(left, zero, qp)]
        s = _dot_nt(jnp.concatenate(stacked, axis=0), kk) + bias
        sink = sink_ref[g]
        m = jnp.maximum(jnp.max(s, axis=-1, keepdims=True), sink)
        p = jnp.exp2(s - m)
        denom = jnp.sum(p, axis=-1, keepdims=True) + jnp.exp2(sink - m)
        o = _dot(p.astype(BF16), vv) / denom
        for j in range(2):
            a = o[(2 * j) * BLOCK:(2 * j + 1) * BLOCK]
            b = o[(2 * j + 1) * BLOCK:(2 * j + 2) * BLOCK]
            left = _half_lane_mask(a.shape)
            o_ref[rows, (2 * g + j) * LANES:(2 * g + j + 1) * LANES] = jnp.where(left, a, b).astype(BF16)

    def with_previous(n, start):
        for g in range(SWA_KV_HEADS):
            cols = slice(g * LANES, (g + 1) * LANES)
            group_block(n, g, kd_ref[pl.ds(start, 2 * BLOCK), cols], vd_ref[pl.ds(start, 2 * BLOCK), cols],
                        tbl_ref[g])

    @pl.when(i == 0)
    def _():
        for g in range(SWA_KV_HEADS):
            cols = slice(g * LANES, (g + 1) * LANES)
            group_block(0, g, kd_ref[0:BLOCK, cols], vd_ref[0:BLOCK, cols], tbl_ref[g, :, BLOCK:])

    @pl.when(i > 0)
    def _():
        with_previous(0, pl.multiple_of(i * ATTN_TILE - BLOCK, BLOCK))

    for n in range(1, blocks):
        with_previous(n, pl.multiple_of(i * ATTN_TILE + (n - 1) * BLOCK, BLOCK))


def _swa_attention(qa, kd, vd, tbl, sink):
    b, s, _ = qa.shape
    tq = ATTN_TILE
    return pl.pallas_call(
        _swa_kernel,
        grid=(b, s // tq),
        in_specs=[pl.BlockSpec((None, tq, SWA_Q_W), lambda bi, i: (bi, i, 0)),
                  pl.BlockSpec((None, s, 2 * SWA_KV_W), lambda bi, i: (bi, 0, 0)),
                  pl.BlockSpec((None, s, 2 * SWA_KV_W), lambda bi, i: (bi, 0, 0)),
                  pl.BlockSpec(tbl.shape, lambda bi, i: (0, 0, 0)),
                  pl.BlockSpec(sink.shape, lambda bi, i: (0, 0, 0))],
        out_specs=pl.BlockSpec((None, tq, SWA_Q_W), lambda bi, i: (bi, i, 0)),
        out_shape=jax.ShapeDtypeStruct((b, s, SWA_Q_W), BF16),
        compiler_params=pltpu.CompilerParams(dimension_semantics=("parallel", "parallel"),
                                             vmem_limit_bytes=VMEM_LIMIT),
        name="swa_attention",
    )(qa, kd, vd, tbl, sink)


def _diff_kernel(lq1_ref, lk1_ref, lq2_ref, lk2_ref, qt_ref, k_ref, vt_ref, tbl_ref, gain_ref, o_ref,
                 m_sc, l_sc, acc_sc, *, lam_init):
    i = pl.program_id(2)
    tq = ATTN_TILE
    qt = qt_ref[...]
    top = lax.broadcasted_iota(jnp.int32, qt.shape, 0) < HEAD_DIM
    zero = jnp.zeros_like(qt)
    qts = (jnp.where(top, qt, zero), jnp.where(top, zero, qt))

    def keys(j):
        return k_ref[pl.ds(pl.multiple_of(j * tq, tq), tq), :]

    def first_step(j, bias):
        kc, vct = keys(j), vt_ref[j]
        for c in range(2):
            s = _dot(kc, qts[c]) + bias
            m = jnp.max(s, axis=0, keepdims=True)
            p = jnp.exp2(s - m)
            m_sc[c] = m
            l_sc[c] = jnp.sum(p, axis=0, keepdims=True)
            acc_sc[c] = _dot(vct, p.astype(BF16))

    def online_step(j, bias):
        kc, vct = keys(j), vt_ref[j]
        for c in range(2):
            s = _dot(kc, qts[c])
            if bias is not None:
                s = s + bias
            m_prev = m_sc[c]
            m_new = jnp.maximum(m_prev, jnp.max(s, axis=0, keepdims=True))
            alpha = jnp.exp2(m_prev - m_new)
            p = jnp.exp2(s - m_new)
            m_sc[c] = m_new
            l_sc[c] = alpha * l_sc[c] + jnp.sum(p, axis=0, keepdims=True)
            acc_sc[c] = alpha * acc_sc[c] + _dot(vct, p.astype(BF16))

    first_step(i, tbl_ref[tq:, :])

    @pl.when(i > 0)
    def _():
        online_step(i - 1, tbl_ref[:tq, :])

    def far_step(j, carry):
        online_step(j, None)
        return carry

    lax.fori_loop(0, i - 1, far_step, 0)

    lam = (jnp.exp(jnp.sum(lq1_ref[...] * lk1_ref[...], axis=-1, keepdims=True))
           - jnp.exp(jnp.sum(lq2_ref[...] * lk2_ref[...], axis=-1, keepdims=True)) + lam_init)
    o = acc_sc[0] / l_sc[0] - lam * (acc_sc[1] / l_sc[1])
    y = o * lax.rsqrt(jnp.mean(o * o, axis=0, keepdims=True) + EPS) * (gain_ref[...] * (1.0 - lam_init))
    o_ref[...] = y.T.astype(BF16)


def _diff_attention(lam_vecs, qbt, kb, vbt, tbl, gain, lam_init):
    b, s, _ = kb.shape
    tq = ATTN_TILE
    n = s // tq
    vec = pl.BlockSpec((1, HEAD_DIM), lambda bi, h, i: (0, 0))
    return pl.pallas_call(
        functools.partial(_diff_kernel, lam_init=lam_init),
        grid=(b, DIFF_HEADS, n),
        in_specs=[vec, vec, vec, vec,
                  pl.BlockSpec((None, None, LANES, tq), lambda bi, h, i: (bi, i, h, 0)),
                  pl.BlockSpec((None, s, LANES), lambda bi, h, i: (bi, 0, h)),
                  pl.BlockSpec((None, n, LANES, tq), lambda bi, h, i: (bi, 0, h, 0)),
                  pl.BlockSpec((None, 2 * tq, tq), lambda bi, h, i: (h, 0, 0)),
                  pl.BlockSpec((DIFF_V_DIM, 1), lambda bi, h, i: (0, 0))],
        out_specs=pl.BlockSpec((None, tq, LANES), lambda bi, h, i: (bi, i, h)),
        out_shape=jax.ShapeDtypeStruct((b, s, DIFF_V_W), BF16),
        scratch_shapes=[pltpu.VMEM((2, 1, tq), F32), pltpu.VMEM((2, 1, tq), F32),
                        pltpu.VMEM((2, DIFF_V_DIM, tq), F32)],
        compiler_params=pltpu.CompilerParams(dimension_semantics=("parallel", "parallel", "arbitrary"),
                                             vmem_limit_bytes=VMEM_LIMIT),
        name="diff_attention",
    )(*lam_vecs, qbt, kb, vbt, tbl, gain)


def _merge_kernel(x_ref, ya_ref, yb_ref, gate_ref, wa_ref, wb_ref, wo_ref, o_ref):
    ga = gate_ref[:, :D_MODEL]
    gb = gate_ref[:, D_MODEL:]
    mixed = ga * _dot(ya_ref[...], wa_ref[...]) + gb * _dot(yb_ref[...], wb_ref[...])
    o_ref[...] = x_ref[...] + _dot(mixed.astype(BF16), wo_ref[...])


def _merge(x2, ya, yb, gates, wa, wb, wo):
    t = x2.shape[0]
    tm = ROW_TILE
    row = lambda w: pl.BlockSpec((tm, w), lambda i: (i, 0))
    full = lambda a: pl.BlockSpec(a.shape, lambda i: (0, 0))
    return pl.pallas_call(
        _merge_kernel,
        grid=(t // tm,),
        in_specs=[row(D_MODEL), row(SWA_Q_W), row(DIFF_V_W), row(GATE_W), full(wa), full(wb), full(wo)],
        out_specs=row(D_MODEL),
        out_shape=jax.ShapeDtypeStruct((t, D_MODEL), F32),
        compiler_params=pltpu.CompilerParams(dimension_semantics=("parallel",),
                                             vmem_limit_bytes=VMEM_LIMIT),
        name="merge",
    )(x2, ya, yb, gates, wa, wb, wo)


def _conv_mlp_kernel(x_ref, halo_ref, g_ref, wup_ref, cw_ref, cb_ref, wdn_ref, o_ref, ug_sc, uv_sc, acc_sc):
    tm = ROW_TILE
    first = pl.program_id(1) == 0

    def normed(x):
        return (x * lax.rsqrt(jnp.mean(x * x, axis=-1, keepdims=True) + EPS) * g_ref[...]).astype(BF16)

    x = x_ref[...]
    h = normed(x)
    h_halo = normed(halo_ref[...])
    keep = jnp.where(first, 0.0, 1.0)

    def conv(u_sc, cols):
        w = cw_ref[:, cols]
        y = (u_sc[SUBLANES - 2:SUBLANES - 2 + tm, :] * w[0:1]
             + u_sc[SUBLANES - 1:SUBLANES - 1 + tm, :] * w[1:2]
             + u_sc[SUBLANES:SUBLANES + tm, :] * w[2:3])
        return y + cb_ref[:, cols]

    for c in range(D_FF // FF_CHUNK):
        gcols = slice(c * FF_CHUNK, (c + 1) * FF_CHUNK)
        vcols = slice(D_FF + c * FF_CHUNK, D_FF + (c + 1) * FF_CHUNK)
        for u_sc, cols in ((ug_sc, gcols), (uv_sc, vcols)):
            u_sc[0:SUBLANES, :] = _dot(h_halo, wup_ref[:, cols]) * keep
            u_sc[SUBLANES:, :] = _dot(h, wup_ref[:, cols])
        ug = conv(ug_sc, gcols)
        uv = conv(uv_sc, vcols)
        act = (ug * jax.nn.sigmoid(ug) * uv).astype(BF16)
        contrib = _dot(act, wdn_ref[gcols, :])
        if c == 0:
            acc_sc[...] = contrib
        else:
            acc_sc[...] += contrib
    o_ref[...] = x + acc_sc[...]


def _conv_mlp(x3, g_ffn, w_up, conv_w, conv_b, w_down):
    b, s, _ = x3.shape
    tm = ROW_TILE
    halo_blocks = tm // SUBLANES
    full = lambda a: pl.BlockSpec(a.shape, lambda bi, i: (0,) * a.ndim)
    return pl.pallas_call(
        _conv_mlp_kernel,
        grid=(b, s // tm),
        in_specs=[pl.BlockSpec((None, tm, D_MODEL), lambda bi, i: (bi, i, 0)),
                  pl.BlockSpec((None, SUBLANES, D_MODEL),
                               lambda bi, i: (bi, jnp.maximum(i * halo_blocks - 1, 0), 0)),
                  full(g_ffn), full(w_up), full(conv_w), full(conv_b), full(w_down)],
        out_specs=pl.BlockSpec((None, tm, D_MODEL), lambda bi, i: (bi, i, 0)),
        out_shape=jax.ShapeDtypeStruct((b, s, D_MODEL), F32),
        scratch_shapes=[pltpu.VMEM((tm + SUBLANES, FF_CHUNK), F32),
                        pltpu.VMEM((tm + SUBLANES, FF_CHUNK), F32),
                        pltpu.VMEM((tm, D_MODEL), F32)],
        compiler_params=pltpu.CompilerParams(dimension_semantics=("parallel", "parallel"),
                                             vmem_limit_bytes=VMEM_LIMIT),
        name="conv_mlp",
    )(x3, x3, g_ffn, w_up, conv_w, conv_b, w_down)


def _swa_bucket_idx():
    rel = BLOCK + np.arange(BLOCK)[:, None] - np.arange(2 * BLOCK)[None, :]
    return np.where((rel >= 0) & (rel < WINDOW), _t5_bucket_np(rel), -1).astype(np.int32)


def _diff_bucket_idx():
    tq = ATTN_TILE
    rel = tq + np.arange(tq)[None, :] - np.arange(2 * tq)[:, None]
    return np.where(rel >= 0, _t5_bucket_np(rel), -1).astype(np.int32)


def _layer(x, l, rel_bias, swa_tbl, diff_tbl, g_mix, w_in, qn_a, kn_a, sinks, qn_b, kn_b, lam_q1, lam_k1,
           lam_q2, lam_k2, subln_b, w_br_a, w_br_b, w_o, g_ffn, w_up, conv_w, conv_b, w_down):
    b, s, _ = x.shape
    t = b * s
    lam_init = 0.8 - 0.6 * math.exp(-0.3 * l)
    q_scale = HEAD_DIM ** -0.5 * LOG2E
    tile = lambda v, reps, scale=1.0: (jnp.tile(v.astype(F32), reps) * scale).reshape(1, -1)

    x2 = x.reshape(t, D_MODEL)
    qa, kd, vd, qbt, kb, vbt, gates = _in_proj(
        x2, g_mix[l].reshape(1, -1), w_in[l].astype(BF16),
        tile(qn_a[l], SWA_Q_HEADS, q_scale), tile(kn_a[l], SWA_KV_HEADS),
        tile(qn_b[l], 2 * DIFF_HEADS, q_scale), tile(kn_b[l], 2 * DIFF_HEADS))

    sink = jnp.repeat(sinks[l].astype(F32) * LOG2E, BLOCK).reshape(SWA_KV_HEADS, SWA_GROUP * BLOCK, 1)
    ya = _swa_attention(qa.reshape(b, s, -1), kd.reshape(b, s, -1), vd.reshape(b, s, -1), swa_tbl, sink)

    lam_vecs = [v[l].astype(F32).reshape(1, HEAD_DIM) for v in (lam_q1, lam_k1, lam_q2, lam_k2)]
    chunks = s // ATTN_TILE
    yb = _diff_attention(lam_vecs, qbt.reshape(b, chunks, DIFF_QK_W, ATTN_TILE), kb.reshape(b, s, -1),
                         vbt.reshape(b, chunks, DIFF_V_W, ATTN_TILE), diff_tbl,
                         subln_b[l].astype(F32).reshape(-1, 1), lam_init)

    x2 = _merge(x2, ya.reshape(t, -1), yb.reshape(t, -1), gates,
                w_br_a[l].astype(BF16), w_br_b[l].astype(BF16), w_o[l].astype(BF16))
    x3 = _conv_mlp(x2.reshape(b, s, D_MODEL), g_ffn[l].reshape(1, -1), w_up[l].astype(BF16),
                   conv_w[l].astype(F32), conv_b[l].astype(F32).reshape(1, -1), w_down[l].astype(BF16))
    return x3


def kernel(x, rel_bias, g_mix, w_in, qn_a, kn_a, sinks, qn_b, kn_b, lam_q1, lam_k1, lam_q2, lam_k2, subln_b,
           w_br_a, w_br_b, w_o, g_ffn, w_up, conv_w, conv_b, w_down):
    rb = rel_bias.astype(F32)
    swa_tbl = _bias_tables(rb, _swa_bucket_idx(), SWA_Q_HEADS, 0, False)
    swa_tbl = swa_tbl.reshape(SWA_KV_HEADS, SWA_GROUP * BLOCK, 2 * BLOCK)
    diff_tbl = _bias_tables(rb, _diff_bucket_idx(), DIFF_HEADS, SWA_Q_HEADS, True)
    for l in range(g_mix.shape[0]):
        x = _layer(x, l, rb, swa_tbl, diff_tbl, g_mix, w_in, qn_a, kn_a, sinks, qn_b, kn_b, lam_q1, lam_k1,
                   lam_q2, lam_k2, subln_b, w_br_a, w_br_b, w_o, g_ffn, w_up, conv_w, conv_b, w_down)
    return x
```

```python
import functools
import math

import numpy as np
import jax
import jax.numpy as jnp
from jax import lax
from jax.experimental import pallas as pl
from jax.experimental.pallas import tpu as pltpu

D_MODEL = 1024
HEAD_DIM = 64
SWA_Q_HEADS = 8
SWA_KV_HEADS = 2
SWA_GROUP = SWA_Q_HEADS // SWA_KV_HEADS
WINDOW = 128
BLOCK = 128
DIFF_HEADS = 4
DIFF_V_DIM = 2 * HEAD_DIM
N_BUCKETS = 32
MAX_DISTANCE = 128
D_FF = 2816
CONV_WIDTH = 3
EPS = 1e-6

SWA_Q_W = SWA_Q_HEADS * HEAD_DIM
SWA_KV_W = SWA_KV_HEADS * HEAD_DIM
DIFF_QK_W = DIFF_HEADS * 2 * HEAD_DIM
DIFF_V_W = DIFF_HEADS * DIFF_V_DIM
GATE_W = 2 * D_MODEL
IN_WIDTH = SWA_Q_W + 2 * SWA_KV_W + 2 * DIFF_QK_W + DIFF_V_W + GATE_W

LANES = 128
SUBLANES = 8
LOG2E = math.log2(math.e)
NEG_INF = float("-inf")

ROW_TILE = 512
ATTN_TILE = 512
FF_CHUNK = 256
VMEM_LIMIT = 56 * 1024 * 1024

F32 = jnp.float32
BF16 = jnp.bfloat16


def _t5_bucket_np(rel):
    n = np.maximum(rel, 0)
    max_exact = N_BUCKETS // 2
    nf = np.maximum(n, 1).astype(np.float64)
    large = max_exact + (np.log(nf / max_exact) / math.log(MAX_DISTANCE / max_exact)
                         * (N_BUCKETS - max_exact)).astype(np.int32)
    large = np.minimum(large, N_BUCKETS - 1)
    return np.where(n < max_exact, n, large).astype(np.int32)


def _dot(a, b):
    return jnp.dot(a, b, preferred_element_type=F32)


def _dot_nt(a, b):
    return lax.dot_general(a, b, (((1,), (1,)), ((), ())), preferred_element_type=F32)


def _half_lane_mask(shape):
    return lax.broadcasted_iota(jnp.int32, shape, len(shape) - 1) < HEAD_DIM


def _bias_table_kernel(rb_ref, idx_ref, out_ref, *, col0, shift_far):
    h = pl.program_id(0) + col0
    idx = idx_ref[...]
    shift = rb_ref[N_BUCKETS - 1, h] if shift_far else 0.0
    acc = jnp.full(idx.shape, NEG_INF, F32)
    for b in range(N_BUCKETS):
        acc = jnp.where(idx == b, (rb_ref[b, h] - shift) * LOG2E, acc)
    out_ref[...] = acc


def _bias_tables(rel_bias, idx, n_heads, col0, shift_far):
    r, c = idx.shape
    return pl.pallas_call(
        functools.partial(_bias_table_kernel, col0=col0, shift_far=shift_far),
        grid=(n_heads,),
        in_specs=[pl.BlockSpec(memory_space=pltpu.SMEM),
                  pl.BlockSpec((r, c), lambda h: (0, 0))],
        out_specs=pl.BlockSpec((None, r, c), lambda h: (h, 0, 0)),
        out_shape=jax.ShapeDtypeStruct((n_heads, r, c), F32),
        name="bias_tables",
    )(rel_bias, jnp.asarray(idx))


def _pair_rms_norm(z, gain):
    outs = []
    for j in range(z.shape[1] // LANES):
        s = z[:, j * LANES:(j + 1) * LANES]
        left = _half_lane_mask(s.shape)
        sq = s * s
        tot = jnp.sum(sq, axis=-1, keepdims=True)
        lo = jnp.sum(jnp.where(left, sq, 0.0), axis=-1, keepdims=True)
        ms = jnp.where(left, lo, tot - lo) * (1.0 / HEAD_DIM)
        outs.append(s * lax.rsqrt(ms + EPS))
    return jnp.concatenate(outs, axis=-1) * gain


def _dup_halves(z):
    left = _half_lane_mask(z.shape)
    rolled = pltpu.roll(z, HEAD_DIM, axis=1)
    return jnp.concatenate([jnp.where(left, z, rolled), jnp.where(left, rolled, z)], axis=-1)


def _in_proj_kernel(x_ref, g_ref, w_ref, gqa_ref, gka_ref, gqb_ref, gkb_ref,
                    qa_ref, kd_ref, vd_ref, qbt_ref, kb_ref, vbt_ref, gate_ref):
    x = x_ref[...]
    h = (x * lax.rsqrt(jnp.mean(x * x, axis=-1, keepdims=True) + EPS) * g_ref[...]).astype(BF16)

    off = 0

    def proj(width):
        nonlocal off
        z = _dot(h, w_ref[:, off:off + width])
        off += width
        return z

    qa_ref[...] = _pair_rms_norm(proj(SWA_Q_W), gqa_ref[...]).astype(BF16)
    kd_ref[...] = _dup_halves(_pair_rms_norm(proj(SWA_KV_W), gka_ref[...])).astype(BF16)
    vd_ref[...] = _dup_halves(proj(SWA_KV_W)).astype(BF16)
    qbt_ref[...] = _pair_rms_norm(proj(DIFF_QK_W), gqb_ref[...]).T.astype(BF16)
    kb_ref[...] = _pair_rms_norm(proj(DIFF_QK_W), gkb_ref[...]).astype(BF16)
    vbt_ref[...] = proj(DIFF_V_W).T.astype(BF16)
    gate_ref[...] = jax.nn.sigmoid(proj(GATE_W))


def _in_proj(x2, g_mix, w_in, gqa, gka, gqb, gkb):
    t = x2.shape[0]
    tm = ATTN_TILE
    row = lambda w: pl.BlockSpec((tm, w), lambda i: (i, 0))
    full = lambda a: pl.BlockSpec(a.shape, lambda i: (0, 0))
    tposed = lambda w: pl.BlockSpec((None, w, tm), lambda i: (i, 0, 0))
    bf = lambda w: jax.ShapeDtypeStruct((t, w), BF16)
    bft = lambda w: jax.ShapeDtypeStruct((t // tm, w, tm), BF16)
    return pl.pallas_call(
        _in_proj_kernel,
        grid=(t // tm,),
        in_specs=[row(D_MODEL), full(g_mix), full(w_in), full(gqa), full(gka), full(gqb), full(gkb)],
        out_specs=[row(SWA_Q_W), row(2 * SWA_KV_W), row(2 * SWA_KV_W), tposed(DIFF_QK_W), row(DIFF_QK_W),
                   tposed(DIFF_V_W), row(GATE_W)],
        out_shape=[bf(SWA_Q_W), bf(2 * SWA_KV_W), bf(2 * SWA_KV_W), bft(DIFF_QK_W), bf(DIFF_QK_W),
                   bft(DIFF_V_W), jax.ShapeDtypeStruct((t, GATE_W), F32)],
        compiler_params=pltpu.CompilerParams(dimension_semantics=("parallel",),
                                             vmem_limit_bytes=VMEM_LIMIT),
        name="in_proj",
    )(x2, g_mix, w_in, gqa, gka, gqb, gkb)


def _swa_kernel(q_ref, kd_ref, vd_ref, tbl_ref, sink_ref, o_ref):
    i = pl.program_id(1)
    blocks = ATTN_TILE // BLOCK

    def group_block(n, g, kk, vv, bias):
        rows = slice(n * BLOCK, (n + 1) * BLOCK)
        stacked = []
        for j in range(2):
            qp = q_ref[rows, (2 * g + j) * LANES:(2 * g + j + 1) * LANES]
            left = _half_lane_mask(qp.shape)
            zero = jnp.zeros_like(qp)
            stacked += [jnp.where(left, qp, zero), jnp.where(left, zero, qp)]
        s = _dot_nt(jnp.concatenate(stacked, axis=0), kk) + bias
        sink = sink_ref[g]
        m = jnp.maximum(jnp.max(s, axis=-1, keepdims=True), sink)
        p = jnp.exp2(s - m)
        denom = jnp.sum(p, axis=-1, keepdims=True) + jnp.exp2(sink - m)
        o = _dot(p.astype(BF16), vv) / denom
        for j in range(2):
            a = o[(2 * j) * BLOCK:(2 * j + 1) * BLOCK]
            b = o[(2 * j + 1) * BLOCK:(2 * j + 2) * BLOCK]
            left = _half_lane_mask(a.shape)
            o_ref[rows, (2 * g + j) * LANES:(2 * g + j + 1) * LANES] = jnp.where(left, a, b).astype(BF16)

    def with_previous(n, start):
        for g in range(SWA_KV_HEADS):
            cols = slice(g * LANES, (g + 1) * LANES)
            group_block(n, g, kd_ref[pl.ds(start, 2 * BLOCK), cols], vd_ref[pl.ds(start, 2 * BLOCK), cols],
                        tbl_ref[g])

    @pl.when(i == 0)
    def _():
        for g in range(SWA_KV_HEADS):
            cols = slice(g * LANES, (g + 1) * LANES)
            group_block(0, g, kd_ref[0:BLOCK, cols], vd_ref[0:BLOCK, cols], tbl_ref[g, :, BLOCK:])

    @pl.when(i > 0)
    def _():
        with_previous(0, pl.multiple_of(i * ATTN_TILE - BLOCK, BLOCK))

    for n in range(1, blocks):
        with_previous(n, pl.multiple_of(i * ATTN_TILE + (n - 1) * BLOCK, BLOCK))


def _swa_attention(qa, kd, vd, tbl, sink):
    b, s, _ = qa.shape
    tq = ATTN_TILE
    return pl.pallas_call(
        _swa_kernel,
        grid=(b, s // tq),
        in_specs=[pl.BlockSpec((None, tq, SWA_Q_W), lambda bi, i: (bi, i, 0)),
                  pl.BlockSpec((None, s, 2 * SWA_KV_W), lambda bi, i: (bi, 0, 0)),
                  pl.BlockSpec((None, s, 2 * SWA_KV_W), lambda bi, i: (bi, 0, 0)),
                  pl.BlockSpec(tbl.shape, lambda bi, i: (0, 0, 0)),
                  pl.BlockSpec(sink.shape, lambda bi, i: (0, 0, 0))],
        out_specs=pl.BlockSpec((None, tq, SWA_Q_W), lambda bi, i: (bi, i, 0)),
        out_shape=jax.ShapeDtypeStruct((b, s, SWA_Q_W), BF16),
        compiler_params=pltpu.CompilerParams(dimension_semantics=("parallel", "parallel"),
                                             vmem_limit_bytes=VMEM_LIMIT),
        name="swa_attention",
    )(qa, kd, vd, tbl, sink)


def _diff_kernel(lq1_ref, lk1_ref, lq2_ref, lk2_ref, qt_ref, k_ref, vt_ref, tbl_ref, gain_ref, o_ref,
                 m_sc, l_sc, acc_sc, s_sc, *, lam_init):
    i = pl.program_id(2)
    tq = ATTN_TILE
    qt = qt_ref[...]
    top = lax.broadcasted_iota(jnp.int32, qt.shape, 0) < HEAD_DIM
    zero = jnp.zeros_like(qt)
    qts = (jnp.where(top, qt, zero), jnp.where(top, zero, qt))

    def scores(j, c):
        kc = k_ref[pl.ds(pl.multiple_of(j * tq, tq), tq), :]
        return _dot(kc, qts[c])

    def step(j, j_next, bias, first):
        vct = vt_ref[j]
        for c in range(2):
            s = s_sc[c]
            if bias is not None:
                s = s + bias
            m_new = jnp.max(s, axis=0, keepdims=True)
            if not first:
                m_prev = m_sc[c]
                m_new = jnp.maximum(m_prev, m_new)
                alpha = jnp.exp2(m_prev - m_new)
            p = jnp.exp2(s - m_new)
            psum = jnp.sum(p, axis=0, keepdims=True)
            s_sc[c] = scores(j_next, c)
            pv = _dot(vct, p.astype(BF16))
            m_sc[c] = m_new
            if first:
                l_sc[c] = psum
                acc_sc[c] = pv
            else:
                l_sc[c] = alpha * l_sc[c] + psum
                acc_sc[c] = alpha * acc_sc[c] + pv

    for c in range(2):
        s_sc[c] = scores(i, c)
    step(i, jnp.maximum(i - 1, 0), tbl_ref[tq:, :], True)

    @pl.when(i > 0)
    def _():
        step(i - 1, jnp.maximum(i - 2, 0), tbl_ref[:tq, :], False)

    def far_step(t, carry):
        j = i - 2 - t
        step(j, jnp.maximum(j - 1, 0), None, False)
        return carry

    lax.fori_loop(0, i - 1, far_step, 0)

    lam = (jnp.exp(jnp.sum(lq1_ref[...] * lk1_ref[...], axis=-1, keepdims=True))
           - jnp.exp(jnp.sum(lq2_ref[...] * lk2_ref[...], axis=-1, keepdims=True)) + lam_init)
    o = acc_sc[0] / l_sc[0] - lam * (acc_sc[1] / l_sc[1])
    y = o * lax.rsqrt(jnp.mean(o * o, axis=0, keepdims=True) + EPS) * (gain_ref[...] * (1.0 - lam_init))
    o_ref[...] = y.T.astype(BF16)


def _diff_attention(lam_vecs, qbt, kb, vbt, tbl, gain, lam_init):
    b, s, _ = kb.shape
    tq = ATTN_TILE
    n = s // tq
    vec = pl.BlockSpec((1, HEAD_DIM), lambda bi, h, i: (0, 0))
    return pl.pallas_call(
        functools.partial(_diff_kernel, lam_init=lam_init),
        grid=(b, DIFF_HEADS, n),
        in_specs=[vec, vec, vec, vec,
                  pl.BlockSpec((None, None, LANES, tq), lambda bi, h, i: (bi, i, h, 0)),
                  pl.BlockSpec((None, s, LANES), lambda bi, h, i: (bi, 0, h)),
                  pl.BlockSpec((None, n, LANES, tq), lambda bi, h, i: (bi, 0, h, 0)),
                  pl.BlockSpec((None, 2 * tq, tq), lambda bi, h, i: (h, 0, 0)),
                  pl.BlockSpec((DIFF_V_DIM, 1), lambda bi, h, i: (0, 0))],
        out_specs=pl.BlockSpec((None, tq, LANES), lambda bi, h, i: (bi, i, h)),
        out_shape=jax.ShapeDtypeStruct((b, s, DIFF_V_W), BF16),
        scratch_shapes=[pltpu.VMEM((2, 1, tq), F32), pltpu.VMEM((2, 1, tq), F32),
                        pltpu.VMEM((2, DIFF_V_DIM, tq), F32), pltpu.VMEM((2, tq, tq), F32)],
        compiler_params=pltpu.CompilerParams(dimension_semantics=("parallel", "parallel", "arbitrary"),
                                             vmem_limit_bytes=VMEM_LIMIT),
        name="diff_attention",
    )(*lam_vecs, qbt, kb, vbt, tbl, gain)


def _merge_kernel(x_ref, ya_ref, yb_ref, gate_ref, wa_ref, wb_ref, wo_ref, o_ref):
    ga = gate_ref[:, :D_MODEL]
    gb = gate_ref[:, D_MODEL:]
    mixed = ga * _dot(ya_ref[...], wa_ref[...]) + gb * _dot(yb_ref[...], wb_ref[...])
    o_ref[...] = x_ref[...] + _dot(mixed.astype(BF16), wo_ref[...])


def _merge(x2, ya, yb, gates, wa, wb, wo):
    t = x2.shape[0]
    tm = ROW_TILE
    row = lambda w: pl.BlockSpec((tm, w), lambda i: (i, 0))
    full = lambda a: pl.BlockSpec(a.shape, lambda i: (0, 0))
    return pl.pallas_call(
        _merge_kernel,
        grid=(t // tm,),
        in_specs=[row(D_MODEL), row(SWA_Q_W), row(DIFF_V_W), row(GATE_W), full(wa), full(wb), full(wo)],
        out_specs=row(D_MODEL),
        out_shape=jax.ShapeDtypeStruct((t, D_MODEL), F32),
        compiler_params=pltpu.CompilerParams(dimension_semantics=("parallel",),
                                             vmem_limit_bytes=VMEM_LIMIT),
        name="merge",
    )(x2, ya, yb, gates, wa, wb, wo)


def _conv_mlp_kernel(x_ref, halo_ref, g_ref, wup_ref, cw_ref, cb_ref, wdn_ref, o_ref, ug_sc, uv_sc, acc_sc):
    tm = ROW_TILE
    first = pl.program_id(1) == 0

    def normed(x):
        return (x * lax.rsqrt(jnp.mean(x * x, axis=-1, keepdims=True) + EPS) * g_ref[...]).astype(BF16)

    x = x_ref[...]
    h = normed(x)
    h_halo = normed(halo_ref[...])
    keep = jnp.where(first, 0.0, 1.0)

    def conv(u_sc, cols):
        w = cw_ref[:, cols]
        y = (u_sc[SUBLANES - 2:SUBLANES - 2 + tm, :] * w[0:1]
             + u_sc[SUBLANES - 1:SUBLANES - 1 + tm, :] * w[1:2]
             + u_sc[SUBLANES:SUBLANES + tm, :] * w[2:3])
        return y + cb_ref[:, cols]

    for c in range(D_FF // FF_CHUNK):
        gcols = slice(c * FF_CHUNK, (c + 1) * FF_CHUNK)
        vcols = slice(D_FF + c * FF_CHUNK, D_FF + (c + 1) * FF_CHUNK)
        for u_sc, cols in ((ug_sc, gcols), (uv_sc, vcols)):
            u_sc[0:SUBLANES, :] = _dot(h_halo, wup_ref[:, cols]) * keep
            u_sc[SUBLANES:, :] = _dot(h, wup_ref[:, cols])
        ug = conv(ug_sc, gcols)
        uv = conv(uv_sc, vcols)
        act = (ug * jax.nn.sigmoid(ug) * uv).astype(BF16)
        contrib = _dot(act, wdn_ref[gcols, :])
        if c == 0:
            acc_sc[...] = contrib
        else:
            acc_sc[...] += contrib
    o_ref[...] = x + acc_sc[...]


def _conv_mlp(x3, g_ffn, w_up, conv_w, conv_b, w_down):
    b, s, _ = x3.shape
    tm = ROW_TILE
    halo_blocks = tm // SUBLANES
    full = lambda a: pl.BlockSpec(a.shape, lambda bi, i: (0,) * a.ndim)
    return pl.pallas_call(
        _conv_mlp_kernel,
        grid=(b, s // tm),
        in_specs=[pl.BlockSpec((None, tm, D_MODEL), lambda bi, i: (bi, i, 0)),
                  pl.BlockSpec((None, SUBLANES, D_MODEL),
                               lambda bi, i: (bi, jnp.maximum(i * halo_blocks - 1, 0), 0)),
                  full(g_ffn), full(w_up), full(conv_w), full(conv_b), full(w_down)],
        out_specs=pl.BlockSpec((None, tm, D_MODEL), lambda bi, i: (bi, i, 0)),
        out_shape=jax.ShapeDtypeStruct((b, s, D_MODEL), F32),
        scratch_shapes=[pltpu.VMEM((tm + SUBLANES, FF_CHUNK), F32),
                        pltpu.VMEM((tm + SUBLANES, FF_CHUNK), F32),
                        pltpu.VMEM((tm, D_MODEL), F32)],
        compiler_params=pltpu.CompilerParams(dimension_semantics=("parallel", "parallel"),
                                             vmem_limit_bytes=VMEM_LIMIT),
        name="conv_mlp",
    )(x3, x3, g_ffn, w_up, conv_w, conv_b, w_down)


def _swa_bucket_idx():
    rel = BLOCK + np.arange(BLOCK)[:, None] - np.arange(2 * BLOCK)[None, :]
    return np.where((rel >= 0) & (rel < WINDOW), _t5_bucket_np(rel), -1).astype(np.int32)


def _diff_bucket_idx():
    tq = ATTN_TILE
    rel = tq + np.arange(tq)[None, :] - np.arange(2 * tq)[:, None]
    return np.where(rel >= 0, _t5_bucket_np(rel), -1).astype(np.int32)


def _layer(x, l, rel_bias, swa_tbl, diff_tbl, g_mix, w_in, qn_a, kn_a, sinks, qn_b, kn_b, lam_q1, lam_k1,
           lam_q2, lam_k2, subln_b, w_br_a, w_br_b, w_o, g_ffn, w_up, conv_w, conv_b, w_down):
    b, s, _ = x.shape
    t = b * s
    lam_init = 0.8 - 0.6 * math.exp(-0.3 * l)
    q_scale = HEAD_DIM ** -0.5 * LOG2E
    tile = lambda v, reps, scale=1.0: (jnp.tile(v.astype(F32), reps) * scale).reshape(1, -1)

    x2 = x.reshape(t, D_MODEL)
    qa, kd, vd, qbt, kb, vbt, gates = _in_proj(
        x2, g_mix[l].reshape(1, -1), w_in[l].astype(BF16),
        tile(qn_a[l], SWA_Q_HEADS, q_scale), tile(kn_a[l], SWA_KV_HEADS),
        tile(qn_b[l], 2 * DIFF_HEADS, q_scale), tile(kn_b[l], 2 * DIFF_HEADS))

    sink = jnp.repeat(sinks[l].astype(F32) * LOG2E, BLOCK).reshape(SWA_KV_HEADS, SWA_GROUP * BLOCK, 1)
    ya = _swa_attention(qa.reshape(b, s, -1), kd.reshape(b, s, -1), vd.reshape(b, s, -1), swa_tbl, sink)

    lam_vecs = [v[l].astype(F32).reshape(1, HEAD_DIM) for v in (lam_q1, lam_k1, lam_q2, lam_k2)]
    chunks = s // ATTN_TILE
    yb = _diff_attention(lam_vecs, qbt.reshape(b, chunks, DIFF_QK_W, ATTN_TILE), kb.reshape(b, s, -1),
                         vbt.reshape(b, chunks, DIFF_V_W, ATTN_TILE), diff_tbl,
                         subln_b[l].astype(F32).reshape(-1, 1), lam_init)

    x2 = _merge(x2, ya.reshape(t, -1), yb.reshape(t, -1), gates,
                w_br_a[l].astype(BF16), w_br_b[l].astype(BF16), w_o[l].astype(BF16))
    x3 = _conv_mlp(x2.reshape(b, s, D_MODEL), g_ffn[l].reshape(1, -1), w_up[l].astype(BF16),
                   conv_w[l].astype(F32), conv_b[l].astype(F32).reshape(1, -1), w_down[l].astype(BF16))
    return x3


def kernel(x, rel_bias, g_mix, w_in, qn_a, kn_a, sinks, qn_b, kn_b, lam_q1, lam_k1, lam_q2, lam_k2, subln_b,
           w_br_a, w_br_b, w_o, g_ffn, w_up, conv_w, conv_b, w_down):
    rb = rel_bias.astype(F32)
    swa_tbl = _bias_tables(rb, _swa_bucket_idx(), SWA_Q_HEADS, 0, False)
    swa_tbl = swa_tbl.reshape(SWA_KV_HEADS, SWA_GROUP * BLOCK, 2 * BLOCK)
    diff_tbl = _bias_tables(rb, _diff_bucket_idx(), DIFF_HEADS, SWA_Q_HEADS, True)
    for l in range(g_mix.shape[0]):
        x = _layer(x, l, rb, swa_tbl, diff_tbl, g_mix, w_in, qn_a, kn_a, sinks, qn_b, kn_b, lam_q1, lam_k1,
                   lam_q2, lam_k2, subln_b, w_br_a, w_br_b, w_o, g_ffn, w_up, conv_w, conv_b, w_down)
    return x
```

```python
import functools
import math

import numpy as np
import jax
import jax.numpy as jnp
from jax import lax
from jax.experimental import pallas as pl
from jax.experimental.pallas import tpu as pltpu

D_MODEL = 1024
HEAD_DIM = 64
SWA_Q_HEADS = 8
SWA_KV_HEADS = 2
SWA_GROUP = SWA_Q_HEADS // SWA_KV_HEADS
WINDOW = 128
BLOCK = 128
DIFF_HEADS = 4
DIFF_V_DIM = 2 * HEAD_DIM
N_BUCKETS = 32
MAX_DISTANCE = 128
D_FF = 2816
CONV_WIDTH = 3
EPS = 1e-6

SWA_Q_W = SWA_Q_HEADS * HEAD_DIM
SWA_KV_W = SWA_KV_HEADS * HEAD_DIM
DIFF_QK_W = DIFF_HEADS * 2 * HEAD_DIM
DIFF_V_W = DIFF_HEADS * DIFF_V_DIM
GATE_W = 2 * D_MODEL
IN_WIDTH = SWA_Q_W + 2 * SWA_KV_W + 2 * DIFF_QK_W + DIFF_V_W + GATE_W

LANES = 128
SUBLANES = 8
LOG2E = math.log2(math.e)
NEG_INF = float("-inf")

ROW_TILE = 512
ATTN_TILE = 512
FF_CHUNK = 256
FAR_UNROLL = 4
VMEM_LIMIT = 56 * 1024 * 1024

F32 = jnp.float32
BF16 = jnp.bfloat16


def _t5_bucket_np(rel):
    n = np.maximum(rel, 0)
    max_exact = N_BUCKETS // 2
    nf = np.maximum(n, 1).astype(np.float64)
    large = max_exact + (np.log(nf / max_exact) / math.log(MAX_DISTANCE / max_exact)
                         * (N_BUCKETS - max_exact)).astype(np.int32)
    large = np.minimum(large, N_BUCKETS - 1)
    return np.where(n < max_exact, n, large).astype(np.int32)


def _dot(a, b):
    return jnp.dot(a, b, preferred_element_type=F32)


def _dot_nt(a, b):
    return lax.dot_general(a, b, (((1,), (1,)), ((), ())), preferred_element_type=F32)


def _half_lane_mask(shape):
    return lax.broadcasted_iota(jnp.int32, shape, len(shape) - 1) < HEAD_DIM


def _bias_table_kernel(rb_ref, idx_ref, out_ref, *, col0, shift_far):
    h = pl.program_id(0) + col0
    idx = idx_ref[...]
    shift = rb_ref[N_BUCKETS - 1, h] if shift_far else 0.0
    acc = jnp.full(idx.shape, NEG_INF, F32)
    for b in range(N_BUCKETS):
        acc = jnp.where(idx == b, (rb_ref[b, h] - shift) * LOG2E, acc)
    out_ref[...] = acc


def _bias_tables(rel_bias, idx, n_heads, col0, shift_far):
    r, c = idx.shape
    return pl.pallas_call(
        functools.partial(_bias_table_kernel, col0=col0, shift_far=shift_far),
        grid=(n_heads,),
        in_specs=[pl.BlockSpec(memory_space=pltpu.SMEM),
                  pl.BlockSpec((r, c), lambda h: (0, 0))],
        out_specs=pl.BlockSpec((None, r, c), lambda h: (h, 0, 0)),
        out_shape=jax.ShapeDtypeStruct((n_heads, r, c), F32),
        name="bias_tables",
    )(rel_bias, jnp.asarray(idx))


def _pair_rms_norm(z, gain):
    outs = []
    for j in range(z.shape[1] // LANES):
        s = z[:, j * LANES:(j + 1) * LANES]
        left = _half_lane_mask(s.shape)
        sq = s * s
        tot = jnp.sum(sq, axis=-1, keepdims=True)
        lo = jnp.sum(jnp.where(left, sq, 0.0), axis=-1, keepdims=True)
        ms = jnp.where(left, lo, tot - lo) * (1.0 / HEAD_DIM)
        outs.append(s * lax.rsqrt(ms + EPS))
    return jnp.concatenate(outs, axis=-1) * gain


def _dup_halves(z):
    left = _half_lane_mask(z.shape)
    rolled = pltpu.roll(z, HEAD_DIM, axis=1)
    return jnp.concatenate([jnp.where(left, z, rolled), jnp.where(left, rolled, z)], axis=-1)


def _in_proj_kernel(x_ref, g_ref, w_ref, gqa_ref, gka_ref, gqb_ref, gkb_ref,
                    qa_ref, kd_ref, vd_ref, qbt_ref, kb_ref, vbt_ref, gate_ref):
    x = x_ref[...]
    h = (x * lax.rsqrt(jnp.mean(x * x, axis=-1, keepdims=True) + EPS) * g_ref[...]).astype(BF16)

    off = 0

    def proj(width):
        nonlocal off
        z = _dot(h, w_ref[:, off:off + width])
        off += width
        return z

    qa_ref[...] = _pair_rms_norm(proj(SWA_Q_W), gqa_ref[...]).astype(BF16)
    kd_ref[...] = _dup_halves(_pair_rms_norm(proj(SWA_KV_W), gka_ref[...])).astype(BF16)
    vd_ref[...] = _dup_halves(proj(SWA_KV_W)).astype(BF16)
    qbt_ref[...] = _pair_rms_norm(proj(DIFF_QK_W), gqb_ref[...]).T.astype(BF16)
    kb_ref[...] = _pair_rms_norm(proj(DIFF_QK_W), gkb_ref[...]).astype(BF16)
    vbt_ref[...] = proj(DIFF_V_W).T.astype(BF16)
    gate_ref[...] = jax.nn.sigmoid(proj(GATE_W))


def _in_proj(x2, g_mix, w_in, gqa, gka, gqb, gkb):
    t = x2.shape[0]
    tm = ATTN_TILE
    row = lambda w: pl.BlockSpec((tm, w), lambda i: (i, 0))
    full = lambda a: pl.BlockSpec(a.shape, lambda i: (0, 0))
    tposed = lambda w: pl.BlockSpec((None, w, tm), lambda i: (i, 0, 0))
    bf = lambda w: jax.ShapeDtypeStruct((t, w), BF16)
    bft = lambda w: jax.ShapeDtypeStruct((t // tm, w, tm), BF16)
    return pl.pallas_call(
        _in_proj_kernel,
        grid=(t // tm,),
        in_specs=[row(D_MODEL), full(g_mix), full(w_in), full(gqa), full(gka), full(gqb), full(gkb)],
        out_specs=[row(SWA_Q_W), row(2 * SWA_KV_W), row(2 * SWA_KV_W), tposed(DIFF_QK_W), row(DIFF_QK_W),
                   tposed(DIFF_V_W), row(GATE_W)],
        out_shape=[bf(SWA_Q_W), bf(2 * SWA_KV_W), bf(2 * SWA_KV_W), bft(DIFF_QK_W), bf(DIFF_QK_W),
                   bft(DIFF_V_W), jax.ShapeDtypeStruct((t, GATE_W), F32)],
        compiler_params=pltpu.CompilerParams(dimension_semantics=("parallel",),
                                             vmem_limit_bytes=VMEM_LIMIT),
        name="in_proj",
    )(x2, g_mix, w_in, gqa, gka, gqb, gkb)


def _swa_kernel(q_ref, kd_ref, vd_ref, tbl_ref, sink_ref, o_ref):
    i = pl.program_id(1)
    blocks = ATTN_TILE // BLOCK

    def group_block(n, g, kk, vv, bias):
        rows = slice(n * BLOCK, (n + 1) * BLOCK)
        stacked = []
        for j in range(2):
            qp = q_ref[rows, (2 * g + j) * LANES:(2 * g + j + 1) * LANES]
            left = _half_lane_mask(qp.shape)
            zero = jnp.zeros_like(qp)
            stacked += [jnp.where(left, qp, zero), jnp.where(left, zero, qp)]
        s = _dot_nt(jnp.concatenate(stacked, axis=0), kk) + bias
        sink = sink_ref[g]
        m = jnp.maximum(jnp.max(s, axis=-1, keepdims=True), sink)
        p = jnp.exp2(s - m)
        denom = jnp.sum(p, axis=-1, keepdims=True) + jnp.exp2(sink - m)
        o = _dot(p.astype(BF16), vv) / denom
        for j in range(2):
            a = o[(2 * j) * BLOCK:(2 * j + 1) * BLOCK]
            b = o[(2 * j + 1) * BLOCK:(2 * j + 2) * BLOCK]
            left = _half_lane_mask(a.shape)
            o_ref[rows, (2 * g + j) * LANES:(2 * g + j + 1) * LANES] = jnp.where(left, a, b).astype(BF16)

    def with_previous(n, start):
        for g in range(SWA_KV_HEADS):
            cols = slice(g * LANES, (g + 1) * LANES)
            group_block(n, g, kd_ref[pl.ds(start, 2 * BLOCK), cols], vd_ref[pl.ds(start, 2 * BLOCK), cols],
                        tbl_ref[g])

    @pl.when(i == 0)
    def _():
        for g in range(SWA_KV_HEADS):
            cols = slice(g * LANES, (g + 1) * LANES)
            group_block(0, g, kd_ref[0:BLOCK, cols], vd_ref[0:BLOCK, cols], tbl_ref[g, :, BLOCK:])

    @pl.when(i > 0)
    def _():
        with_previous(0, pl.multiple_of(i * ATTN_TILE - BLOCK, BLOCK))

    for n in range(1, blocks):
        with_previous(n, pl.multiple_of(i * ATTN_TILE + (n - 1) * BLOCK, BLOCK))


def _swa_attention(qa, kd, vd, tbl, sink):
    b, s, _ = qa.shape
    tq = ATTN_TILE
    return pl.pallas_call(
        _swa_kernel,
        grid=(b, s // tq),
        in_specs=[pl.BlockSpec((None, tq, SWA_Q_W), lambda bi, i: (bi, i, 0)),
                  pl.BlockSpec((None, s, 2 * SWA_KV_W), lambda bi, i: (bi, 0, 0)),
                  pl.BlockSpec((None, s, 2 * SWA_KV_W), lambda bi, i: (bi, 0, 0)),
                  pl.BlockSpec(tbl.shape, lambda bi, i: (0, 0, 0)),
                  pl.BlockSpec(sink.shape, lambda bi, i: (0, 0, 0))],
        out_specs=pl.BlockSpec((None, tq, SWA_Q_W), lambda bi, i: (bi, i, 0)),
        out_shape=jax.ShapeDtypeStruct((b, s, SWA_Q_W), BF16),
        compiler_params=pltpu.CompilerParams(dimension_semantics=("parallel", "parallel"),
                                             vmem_limit_bytes=VMEM_LIMIT),
        name="swa_attention",
    )(qa, kd, vd, tbl, sink)


def _diff_kernel(lq1_ref, lk1_ref, lq2_ref, lk2_ref, qt_ref, k_ref, vt_ref, tbl_ref, gain_ref, o_ref,
                 m_sc, l_sc, acc_sc, s_sc, smax_sc, *, lam_init):
    i = pl.program_id(2)
    tq = ATTN_TILE
    qt = qt_ref[...]
    top = lax.broadcasted_iota(jnp.int32, qt.shape, 0) < HEAD_DIM
    zero = jnp.zeros_like(qt)
    qts = (jnp.where(top, qt, zero), jnp.where(top, zero, qt))

    def prefetch_scores(j, c):
        kc = k_ref[pl.ds(pl.multiple_of(j * tq, tq), tq), :]
        s = _dot(kc, qts[c])
        s_sc[c] = s
        smax_sc[c] = jnp.max(s, axis=0, keepdims=True)

    def step(j, j_next, bias, first):
        vct = vt_ref[j]
        for c in range(2):
            s = s_sc[c]
            if bias is not None:
                s = s + bias
                m_new = jnp.max(s, axis=0, keepdims=True)
            else:
                m_new = smax_sc[c]
            if not first:
                m_prev = m_sc[c]
                m_new = jnp.maximum(m_prev, m_new)
                alpha = jnp.exp2(m_prev - m_new)
            p = jnp.exp2(s - m_new)
            psum = jnp.sum(p, axis=0, keepdims=True)
            prefetch_scores(j_next, c)
            pv = _dot(vct, p.astype(BF16))
            m_sc[c] = m_new
            if first:
                l_sc[c] = psum
                acc_sc[c] = pv
            else:
                l_sc[c] = alpha * l_sc[c] + psum
                acc_sc[c] = alpha * acc_sc[c] + pv

    for c in range(2):
        prefetch_scores(i, c)
    step(i, jnp.maximum(i - 1, 0), tbl_ref[tq:, :], True)

    @pl.when(i > 0)
    def _():
        step(i - 1, jnp.maximum(i - 2, 0), tbl_ref[:tq, :], False)

    n_far = jnp.maximum(i - 1, 0)
    n_single = n_far & (FAR_UNROLL - 1)

    def far_single(t, carry):
        j = i - 2 - t
        step(j, jnp.maximum(j - 1, 0), None, False)
        return carry

    lax.fori_loop(0, n_single, far_single, 0)

    def far_group(t, carry):
        j = i - 2 - n_single - FAR_UNROLL * t
        for u in range(FAR_UNROLL):
            step(j - u, jnp.maximum(j - u - 1, 0), None, False)
        return carry

    lax.fori_loop(0, lax.shift_right_logical(n_far, FAR_UNROLL.bit_length() - 1), far_group, 0)

    lam = (jnp.exp(jnp.sum(lq1_ref[...] * lk1_ref[...], axis=-1, keepdims=True))
           - jnp.exp(jnp.sum(lq2_ref[...] * lk2_ref[...], axis=-1, keepdims=True)) + lam_init)
    o = acc_sc[0] / l_sc[0] - lam * (acc_sc[1] / l_sc[1])
    y = o * lax.rsqrt(jnp.mean(o * o, axis=0, keepdims=True) + EPS) * (gain_ref[...] * (1.0 - lam_init))
    o_ref[...] = y.T.astype(BF16)


def _diff_attention(lam_vecs, qbt, kb, vbt, tbl, gain, lam_init):
    b, s, _ = kb.shape
    tq = ATTN_TILE
    n = s // tq
    vec = pl.BlockSpec((1, HEAD_DIM), lambda bi, h, i: (0, 0))
    return pl.pallas_call(
        functools.partial(_diff_kernel, lam_init=lam_init),
        grid=(b, DIFF_HEADS, n),
        in_specs=[vec, vec, vec, vec,
                  pl.BlockSpec((None, None, LANES, tq), lambda bi, h, i: (bi, i, h, 0)),
                  pl.BlockSpec((None, s, LANES), lambda bi, h, i: (bi, 0, h)),
                  pl.BlockSpec((None, n, LANES, tq), lambda bi, h, i: (bi, 0, h, 0)),
                  pl.BlockSpec((None, 2 * tq, tq), lambda bi, h, i: (h, 0, 0)),
                  pl.BlockSpec((DIFF_V_DIM, 1), lambda bi, h, i: (0, 0))],
        out_specs=pl.BlockSpec((None, tq, LANES), lambda bi, h, i: (bi, i, h)),
        out_shape=jax.ShapeDtypeStruct((b, s, DIFF_V_W), BF16),
        scratch_shapes=[pltpu.VMEM((2, 1, tq), F32), pltpu.VMEM((2, 1, tq), F32),
                        pltpu.VMEM((2, DIFF_V_DIM, tq), F32), pltpu.VMEM((2, tq, tq), F32),
                        pltpu.VMEM((2, 1, tq), F32)],
        compiler_params=pltpu.CompilerParams(dimension_semantics=("parallel", "parallel", "arbitrary"),
                                             vmem_limit_bytes=VMEM_LIMIT),
        name="diff_attention",
    )(*lam_vecs, qbt, kb, vbt, tbl, gain)


def _merge_kernel(x_ref, ya_ref, yb_ref, gate_ref, wa_ref, wb_ref, wo_ref, o_ref):
    ga = gate_ref[:, :D_MODEL]
    gb = gate_ref[:, D_MODEL:]
    mixed = ga * _dot(ya_ref[...], wa_ref[...]) + gb * _dot(yb_ref[...], wb_ref[...])
    o_ref[...] = x_ref[...] + _dot(mixed.astype(BF16), wo_ref[...])


def _merge(x2, ya, yb, gates, wa, wb, wo):
    t = x2.shape[0]
    tm = ROW_TILE
    row = lambda w: pl.BlockSpec((tm, w), lambda i: (i, 0))
    full = lambda a: pl.BlockSpec(a.shape, lambda i: (0, 0))
    return pl.pallas_call(
        _merge_kernel,
        grid=(t // tm,),
        in_specs=[row(D_MODEL), row(SWA_Q_W), row(DIFF_V_W), row(GATE_W), full(wa), full(wb), full(wo)],
        out_specs=row(D_MODEL),
        out_shape=jax.ShapeDtypeStruct((t, D_MODEL), F32),
        compiler_params=pltpu.CompilerParams(dimension_semantics=("parallel",),
                                             vmem_limit_bytes=VMEM_LIMIT),
        name="merge",
    )(x2, ya, yb, gates, wa, wb, wo)


def _conv_mlp_kernel(x_ref, halo_ref, g_ref, wup_ref, cw_ref, cb_ref, wdn_ref, o_ref, ug_sc, uv_sc, acc_sc):
    tm = ROW_TILE
    first = pl.program_id(1) == 0

    def normed(x):
        return (x * lax.rsqrt(jnp.mean(x * x, axis=-1, keepdims=True) + EPS) * g_ref[...]).astype(BF16)

    x = x_ref[...]
    h = normed(x)
    h_halo = normed(halo_ref[...])
    keep = jnp.where(first, 0.0, 1.0)

    def conv(u_sc, cols):
        w = cw_ref[:, cols]
        y = (u_sc[SUBLANES - 2:SUBLANES - 2 + tm, :] * w[0:1]
             + u_sc[SUBLANES - 1:SUBLANES - 1 + tm, :] * w[1:2]
             + u_sc[SUBLANES:SUBLANES + tm, :] * w[2:3])
        return y + cb_ref[:, cols]

    for c in range(D_FF // FF_CHUNK):
        gcols = slice(c * FF_CHUNK, (c + 1) * FF_CHUNK)
        vcols = slice(D_FF + c * FF_CHUNK, D_FF + (c + 1) * FF_CHUNK)
        for u_sc, cols in ((ug_sc, gcols), (uv_sc, vcols)):
            u_sc[0:SUBLANES, :] = _dot(h_halo, wup_ref[:, cols]) * keep
            u_sc[SUBLANES:, :] = _dot(h, wup_ref[:, cols])
        ug = conv(ug_sc, gcols)
        uv = conv(uv_sc, vcols)
        act = (ug * jax.nn.sigmoid(ug) * uv).astype(BF16)
        contrib = _dot(act, wdn_ref[gcols, :])
        if c == 0:
            acc_sc[...] = contrib
        else:
            acc_sc[...] += contrib
    o_ref[...] = x + acc_sc[...]


def _conv_mlp(x3, g_ffn, w_up, conv_w, conv_b, w_down):
    b, s, _ = x3.shape
    tm = ROW_TILE
    halo_blocks = tm // SUBLANES
    full = lambda a: pl.BlockSpec(a.shape, lambda bi, i: (0,) * a.ndim)
    return pl.pallas_call(
        _conv_mlp_kernel,
        grid=(b, s // tm),
        in_specs=[pl.BlockSpec((None, tm, D_MODEL), lambda bi, i: (bi, i, 0)),
                  pl.BlockSpec((None, SUBLANES, D_MODEL),
                               lambda bi, i: (bi, jnp.maximum(i * halo_blocks - 1, 0), 0)),
                  full(g_ffn), full(w_up), full(conv_w), full(conv_b), full(w_down)],
        out_specs=pl.BlockSpec((None, tm, D_MODEL), lambda bi, i: (bi, i, 0)),
        out_shape=jax.ShapeDtypeStruct((b, s, D_MODEL), F32),
        scratch_shapes=[pltpu.VMEM((tm + SUBLANES, FF_CHUNK), F32),
                        pltpu.VMEM((tm + SUBLANES, FF_CHUNK), F32),
                        pltpu.VMEM((tm, D_MODEL), F32)],
        compiler_params=pltpu.CompilerParams(dimension_semantics=("parallel", "parallel"),
                                             vmem_limit_bytes=VMEM_LIMIT),
        name="conv_mlp",
    )(x3, x3, g_ffn, w_up, conv_w, conv_b, w_down)


def _swa_bucket_idx():
    rel = BLOCK + np.arange(BLOCK)[:, None] - np.arange(2 * BLOCK)[None, :]
    return np.where((rel >= 0) & (rel < WINDOW), _t5_bucket_np(rel), -1).astype(np.int32)


def _diff_bucket_idx():
    tq = ATTN_TILE
    rel = tq + np.arange(tq)[None, :] - np.arange(2 * tq)[:, None]
    return np.where(rel >= 0, _t5_bucket_np(rel), -1).astype(np.int32)


def _layer(x, l, rel_bias, swa_tbl, diff_tbl, g_mix, w_in, qn_a, kn_a, sinks, qn_b, kn_b, lam_q1, lam_k1,
           lam_q2, lam_k2, subln_b, w_br_a, w_br_b, w_o, g_ffn, w_up, conv_w, conv_b, w_down):
    b, s, _ = x.shape
    t = b * s
    lam_init = 0.8 - 0.6 * math.exp(-0.3 * l)
    q_scale = HEAD_DIM ** -0.5 * LOG2E
    tile = lambda v, reps, scale=1.0: (jnp.tile(v.astype(F32), reps) * scale).reshape(1, -1)

    x2 = x.reshape(t, D_MODEL)
    qa, kd, vd, qbt, kb, vbt, gates = _in_proj(
        x2, g_mix[l].reshape(1, -1), w_in[l].astype(BF16),
        tile(qn_a[l], SWA_Q_HEADS, q_scale), tile(kn_a[l], SWA_KV_HEADS),
        tile(qn_b[l], 2 * DIFF_HEADS, q_scale), tile(kn_b[l], 2 * DIFF_HEADS))

    sink = jnp.repeat(sinks[l].astype(F32) * LOG2E, BLOCK).reshape(SWA_KV_HEADS, SWA_GROUP * BLOCK, 1)
    ya = _swa_attention(qa.reshape(b, s, -1), kd.reshape(b, s, -1), vd.reshape(b, s, -1), swa_tbl, sink)

    lam_vecs = [v[l].astype(F32).reshape(1, HEAD_DIM) for v in (lam_q1, lam_k1, lam_q2, lam_k2)]
    chunks = s // ATTN_TILE
    yb = _diff_attention(lam_vecs, qbt.reshape(b, chunks, DIFF_QK_W, ATTN_TILE), kb.reshape(b, s, -1),
                         vbt.reshape(b, chunks, DIFF_V_W, ATTN_TILE), diff_tbl,
                         subln_b[l].astype(F32).reshape(-1, 1), lam_init)

    x2 = _merge(x2, ya.reshape(t, -1), yb.reshape(t, -1), gates,
                w_br_a[l].astype(BF16), w_br_b[l].astype(BF16), w_o[l].astype(BF16))
    x3 = _conv_mlp(x2.reshape(b, s, D_MODEL), g_ffn[l].reshape(1, -1), w_up[l].astype(BF16),
                   conv_w[l].astype(F32), conv_b[l].astype(F32).reshape(1, -1), w_down[l].astype(BF16))
    return x3


def kernel(x, rel_bias, g_mix, w_in, qn_a, kn_a, sinks, qn_b, kn_b, lam_q1, lam_k1, lam_q2, lam_k2, subln_b,
           w_br_a, w_br_b, w_o, g_ffn, w_up, conv_w, conv_b, w_down):
    rb = rel_bias.astype(F32)
    swa_tbl = _bias_tables(rb, _swa_bucket_idx(), SWA_Q_HEADS, 0, False)
    swa_tbl = swa_tbl.reshape(SWA_KV_HEADS, SWA_GROUP * BLOCK, 2 * BLOCK)
    diff_tbl = _bias_tables(rb, _diff_bucket_idx(), DIFF_HEADS, SWA_Q_HEADS, True)
    for l in range(g_mix.shape[0]):
        x = _layer(x, l, rb, swa_tbl, diff_tbl, g_mix, w_in, qn_a, kn_a, sinks, qn_b, kn_b, lam_q1, lam_k1,
                   lam_q2, lam_k2, subln_b, w_br_a, w_br_b, w_o, g_ffn, w_up, conv_w, conv_b, w_down)
    return x
```

```python
import functools
import math

import numpy as np
import jax
import jax.numpy as jnp
from jax import lax
from jax.experimental import pallas as pl
from jax.experimental.pallas import tpu as pltpu

D_MODEL = 1024
HEAD_DIM = 64
SWA_Q_HEADS = 8
SWA_KV_HEADS = 2
SWA_GROUP = SWA_Q_HEADS // SWA_KV_HEADS
WINDOW = 128
BLOCK = 128
DIFF_HEADS = 4
DIFF_V_DIM = 2 * HEAD_DIM
N_BUCKETS = 32
MAX_DISTANCE = 128
D_FF = 2816
CONV_WIDTH = 3
EPS = 1e-6

SWA_Q_W = SWA_Q_HEADS * HEAD_DIM
SWA_KV_W = SWA_KV_HEADS * HEAD_DIM
DIFF_QK_W = DIFF_HEADS * 2 * HEAD_DIM
DIFF_V_W = DIFF_HEADS * DIFF_V_DIM
GATE_W = 2 * D_MODEL
IN_WIDTH = SWA_Q_W + 2 * SWA_KV_W + 2 * DIFF_QK_W + DIFF_V_W + GATE_W

LANES = 128
SUBLANES = 8
LOG2E = math.log2(math.e)
NEG_INF = float("-inf")

ROW_TILE = 512
MLP_TILE = 512
ATTN_TILE = 512
FF_CHUNK = 256
FAR_UNROLL = 4
VMEM_LIMIT = 56 * 1024 * 1024

F32 = jnp.float32
BF16 = jnp.bfloat16


def _t5_bucket_np(rel):
    n = np.maximum(rel, 0)
    max_exact = N_BUCKETS // 2
    nf = np.maximum(n, 1).astype(np.float64)
    large = max_exact + (np.log(nf / max_exact) / math.log(MAX_DISTANCE / max_exact)
                         * (N_BUCKETS - max_exact)).astype(np.int32)
    large = np.minimum(large, N_BUCKETS - 1)
    return np.where(n < max_exact, n, large).astype(np.int32)


def _dot(a, b):
    return jnp.dot(a, b, preferred_element_type=F32)


def _dot_nt(a, b):
    return lax.dot_general(a, b, (((1,), (1,)), ((), ())), preferred_element_type=F32)


def _half_lane_mask(shape):
    return lax.broadcasted_iota(jnp.int32, shape, len(shape) - 1) < HEAD_DIM


def _bias_table_kernel(rb_ref, idx_ref, out_ref, *, col0, shift_far):
    h = pl.program_id(0) + col0
    idx = idx_ref[...]
    shift = rb_ref[N_BUCKETS - 1, h] if shift_far else 0.0
    acc = jnp.full(idx.shape, NEG_INF, F32)
    for b in range(N_BUCKETS):
        acc = jnp.where(idx == b, (rb_ref[b, h] - shift) * LOG2E, acc)
    out_ref[...] = acc


def _bias_tables(rel_bias, idx, n_heads, col0, shift_far):
    r, c = idx.shape
    return pl.pallas_call(
        functools.partial(_bias_table_kernel, col0=col0, shift_far=shift_far),
        grid=(n_heads,),
        in_specs=[pl.BlockSpec(memory_space=pltpu.SMEM),
                  pl.BlockSpec((r, c), lambda h: (0, 0))],
        out_specs=pl.BlockSpec((None, r, c), lambda h: (h, 0, 0)),
        out_shape=jax.ShapeDtypeStruct((n_heads, r, c), F32),
        name="bias_tables",
    )(rel_bias, jnp.asarray(idx))


def _pair_rms_norm(z, gain):
    outs = []
    for j in range(z.shape[1] // LANES):
        s = z[:, j * LANES:(j + 1) * LANES]
        left = _half_lane_mask(s.shape)
        sq = s * s
        tot = jnp.sum(sq, axis=-1, keepdims=True)
        lo = jnp.sum(jnp.where(left, sq, 0.0), axis=-1, keepdims=True)
        ms = jnp.where(left, lo, tot - lo) * (1.0 / HEAD_DIM)
        outs.append(s * lax.rsqrt(ms + EPS))
    return jnp.concatenate(outs, axis=-1) * gain


def _dup_halves(z):
    left = _half_lane_mask(z.shape)
    rolled = pltpu.roll(z, HEAD_DIM, axis=1)
    return jnp.concatenate([jnp.where(left, z, rolled), jnp.where(left, rolled, z)], axis=-1)


def _in_proj_kernel(x_ref, g_ref, w_ref, gqa_ref, gka_ref, gqb_ref, gkb_ref,
                    qa_ref, kd_ref, vd_ref, qbt_ref, kb_ref, vbt_ref, gate_ref):
    x = x_ref[...]
    h = (x * lax.rsqrt(jnp.mean(x * x, axis=-1, keepdims=True) + EPS) * g_ref[...]).astype(BF16)

    off = 0

    def proj(width):
        nonlocal off
        z = _dot(h, w_ref[:, off:off + width])
        off += width
        return z

    qa_ref[...] = _pair_rms_norm(proj(SWA_Q_W), gqa_ref[...]).astype(BF16)
    kd_ref[...] = _dup_halves(_pair_rms_norm(proj(SWA_KV_W), gka_ref[...])).astype(BF16)
    vd_ref[...] = _dup_halves(proj(SWA_KV_W)).astype(BF16)
    qbt_ref[...] = _pair_rms_norm(proj(DIFF_QK_W), gqb_ref[...]).T.astype(BF16)
    kb_ref[...] = _pair_rms_norm(proj(DIFF_QK_W), gkb_ref[...]).astype(BF16)
    vbt_ref[...] = proj(DIFF_V_W).T.astype(BF16)
    gate_ref[...] = jax.nn.sigmoid(proj(GATE_W))


def _in_proj(x2, g_mix, w_in, gqa, gka, gqb, gkb):
    t = x2.shape[0]
    tm = ATTN_TILE
    row = lambda w: pl.BlockSpec((tm, w), lambda i: (i, 0))
    full = lambda a: pl.BlockSpec(a.shape, lambda i: (0, 0), pipeline_mode=pl.Buffered(1))
    tposed = lambda w: pl.BlockSpec((None, w, tm), lambda i: (i, 0, 0))
    bf = lambda w: jax.ShapeDtypeStruct((t, w), BF16)
    bft = lambda w: jax.ShapeDtypeStruct((t // tm, w, tm), BF16)
    return pl.pallas_call(
        _in_proj_kernel,
        grid=(t // tm,),
        in_specs=[row(D_MODEL), full(g_mix), full(w_in), full(gqa), full(gka), full(gqb), full(gkb)],
        out_specs=[row(SWA_Q_W), row(2 * SWA_KV_W), row(2 * SWA_KV_W), tposed(DIFF_QK_W), row(DIFF_QK_W),
                   tposed(DIFF_V_W), row(GATE_W)],
        out_shape=[bf(SWA_Q_W), bf(2 * SWA_KV_W), bf(2 * SWA_KV_W), bft(DIFF_QK_W), bf(DIFF_QK_W),
                   bft(DIFF_V_W), jax.ShapeDtypeStruct((t, GATE_W), F32)],
        compiler_params=pltpu.CompilerParams(dimension_semantics=("parallel",),
                                             vmem_limit_bytes=VMEM_LIMIT),
        name="in_proj",
    )(x2, g_mix, w_in, gqa, gka, gqb, gkb)


def _swa_kernel(q_ref, kd_ref, vd_ref, tbl_ref, sink_ref, o_ref):
    i = pl.program_id(1)
    blocks = ATTN_TILE // BLOCK

    def group_block(n, g, kk, vv, key_rows):
        rows = slice(n * BLOCK, (n + 1) * BLOCK)
        stacked = []
        for j in range(2):
            qp = q_ref[rows, (2 * g + j) * LANES:(2 * g + j + 1) * LANES]
            left = _half_lane_mask(qp.shape)
            zero = jnp.zeros_like(qp)
            stacked += [jnp.where(left, qp, zero), jnp.where(left, zero, qp)]
        bias = jnp.concatenate([tbl_ref[SWA_GROUP * g + j, key_rows, :] for j in range(SWA_GROUP)], axis=1)
        s = _dot_nt(kk, jnp.concatenate(stacked, axis=0)) + bias
        sink = sink_ref[g]
        m = jnp.maximum(jnp.max(s, axis=0, keepdims=True), sink)
        p = jnp.exp2(s - m)
        denom = jnp.sum(p, axis=0, keepdims=True) + jnp.exp2(sink - m)
        p = (p * (1.0 / denom)).astype(BF16)
        o = lax.dot_general(p, vv, (((0,), (0,)), ((), ())), preferred_element_type=F32)
        for j in range(2):
            a = o[(2 * j) * BLOCK:(2 * j + 1) * BLOCK]
            b = o[(2 * j + 1) * BLOCK:(2 * j + 2) * BLOCK]
            left = _half_lane_mask(a.shape)
            o_ref[rows, (2 * g + j) * LANES:(2 * g + j + 1) * LANES] = jnp.where(left, a, b).astype(BF16)

    def with_previous(n, start):
        for g in range(SWA_KV_HEADS):
            cols = slice(g * LANES, (g + 1) * LANES)
            group_block(n, g, kd_ref[pl.ds(start, 2 * BLOCK), cols], vd_ref[pl.ds(start, 2 * BLOCK), cols],
                        slice(0, 2 * BLOCK))

    @pl.when(i == 0)
    def _():
        for g in range(SWA_KV_HEADS):
            cols = slice(g * LANES, (g + 1) * LANES)
            group_block(0, g, kd_ref[0:BLOCK, cols], vd_ref[0:BLOCK, cols], slice(BLOCK, 2 * BLOCK))

    @pl.when(i > 0)
    def _():
        with_previous(0, pl.multiple_of(i * ATTN_TILE - BLOCK, BLOCK))

    for n in range(1, blocks):
        with_previous(n, pl.multiple_of(i * ATTN_TILE + (n - 1) * BLOCK, BLOCK))


def _swa_attention(qa, kd, vd, tbl, sink):
    b, s, _ = qa.shape
    tq = ATTN_TILE
    return pl.pallas_call(
        _swa_kernel,
        grid=(b, s // tq),
        in_specs=[pl.BlockSpec((None, tq, SWA_Q_W), lambda bi, i: (bi, i, 0)),
                  pl.BlockSpec((None, s, 2 * SWA_KV_W), lambda bi, i: (bi, 0, 0)),
                  pl.BlockSpec((None, s, 2 * SWA_KV_W), lambda bi, i: (bi, 0, 0)),
                  pl.BlockSpec(tbl.shape, lambda bi, i: (0, 0, 0)),
                  pl.BlockSpec(sink.shape, lambda bi, i: (0, 0, 0))],
        out_specs=pl.BlockSpec((None, tq, SWA_Q_W), lambda bi, i: (bi, i, 0)),
        out_shape=jax.ShapeDtypeStruct((b, s, SWA_Q_W), BF16),
        compiler_params=pltpu.CompilerParams(dimension_semantics=("parallel", "parallel"),
                                             vmem_limit_bytes=VMEM_LIMIT),
        name="swa_attention",
    )(qa, kd, vd, tbl, sink)


def _diff_kernel(lq1_ref, lk1_ref, lq2_ref, lk2_ref, qt_ref, k_ref, vt_ref, tbl_ref, gain_ref, o_ref,
                 m_sc, l_sc, acc_sc, s_sc, smax_sc, *, lam_init):
    i = pl.program_id(2)
    tq = ATTN_TILE
    qt = qt_ref[...]
    top = lax.broadcasted_iota(jnp.int32, qt.shape, 0) < HEAD_DIM
    zero = jnp.zeros_like(qt)
    qts = (jnp.where(top, qt, zero), jnp.where(top, zero, qt))

    def prefetch_scores(j, c):
        kc = k_ref[pl.ds(pl.multiple_of(j * tq, tq), tq), :]
        s = _dot(kc, qts[c])
        s_sc[c] = s
        smax_sc[c] = jnp.max(s, axis=0, keepdims=True)

    def step(j, j_next, bias, first):
        vct = vt_ref[j]
        for c in range(2):
            s = s_sc[c]
            if bias is not None:
                s = s + bias
                m_new = jnp.max(s, axis=0, keepdims=True)
            else:
                m_new = smax_sc[c]
            if not first:
                m_prev = m_sc[c]
                m_new = jnp.maximum(m_prev, m_new)
                alpha = jnp.exp2(m_prev - m_new)
            p = jnp.exp2(s - m_new)
            psum = jnp.sum(p, axis=0, keepdims=True)
            prefetch_scores(j_next, c)
            pv = _dot(vct, p.astype(BF16))
            m_sc[c] = m_new
            if first:
                l_sc[c] = psum
                acc_sc[c] = pv
            else:
                l_sc[c] = alpha * l_sc[c] + psum
                acc_sc[c] = alpha * acc_sc[c] + pv

    for c in range(2):
        prefetch_scores(i, c)
    step(i, jnp.maximum(i - 1, 0), tbl_ref[tq:, :], True)

    @pl.when(i > 0)
    def _():
        step(i - 1, jnp.maximum(i - 2, 0), tbl_ref[:tq, :], False)

    n_far = jnp.maximum(i - 1, 0)
    n_single = n_far & (FAR_UNROLL - 1)

    def far_single(t, carry):
        j = i - 2 - t
        step(j, jnp.maximum(j - 1, 0), None, False)
        return carry

    lax.fori_loop(0, n_single, far_single, 0)

    def far_group(t, carry):
        j = i - 2 - n_single - FAR_UNROLL * t
        for u in range(FAR_UNROLL):
            step(j - u, jnp.maximum(j - u - 1, 0), None, False)
        return carry

    lax.fori_loop(0, lax.shift_right_logical(n_far, FAR_UNROLL.bit_length() - 1), far_group, 0)

    lam = (jnp.exp(jnp.sum(lq1_ref[...] * lk1_ref[...], axis=-1, keepdims=True))
           - jnp.exp(jnp.sum(lq2_ref[...] * lk2_ref[...], axis=-1, keepdims=True)) + lam_init)
    o = acc_sc[0] / l_sc[0] - lam * (acc_sc[1] / l_sc[1])
    y = o * lax.rsqrt(jnp.mean(o * o, axis=0, keepdims=True) + EPS) * (gain_ref[...] * (1.0 - lam_init))
    o_ref[...] = y.T.astype(BF16)


def _diff_attention(lam_vecs, qbt, kb, vbt, tbl, gain, lam_init):
    b, s, _ = kb.shape
    tq = ATTN_TILE
    n = s // tq
    vec = pl.BlockSpec((1, HEAD_DIM), lambda bi, h, i: (0, 0))
    return pl.pallas_call(
        functools.partial(_diff_kernel, lam_init=lam_init),
        grid=(b, DIFF_HEADS, n),
        in_specs=[vec, vec, vec, vec,
                  pl.BlockSpec((None, None, LANES, tq), lambda bi, h, i: (bi, i, h, 0)),
                  pl.BlockSpec((None, s, LANES), lambda bi, h, i: (bi, 0, h)),
                  pl.BlockSpec((None, n, LANES, tq), lambda bi, h, i: (bi, 0, h, 0)),
                  pl.BlockSpec((None, 2 * tq, tq), lambda bi, h, i: (h, 0, 0)),
                  pl.BlockSpec((DIFF_V_DIM, 1), lambda bi, h, i: (0, 0))],
        out_specs=pl.BlockSpec((None, tq, LANES), lambda bi, h, i: (bi, i, h)),
        out_shape=jax.ShapeDtypeStruct((b, s, DIFF_V_W), BF16),
        scratch_shapes=[pltpu.VMEM((2, 1, tq), F32), pltpu.VMEM((2, 1, tq), F32),
                        pltpu.VMEM((2, DIFF_V_DIM, tq), F32), pltpu.VMEM((2, tq, tq), F32),
                        pltpu.VMEM((2, 1, tq), F32)],
        compiler_params=pltpu.CompilerParams(dimension_semantics=("parallel", "parallel", "arbitrary"),
                                             vmem_limit_bytes=VMEM_LIMIT),
        name="diff_attention",
    )(*lam_vecs, qbt, kb, vbt, tbl, gain)


def _merge_kernel(x_ref, ya_ref, yb_ref, gate_ref, wa_ref, wb_ref, wo_ref, o_ref):
    ga = gate_ref[:, :D_MODEL]
    gb = gate_ref[:, D_MODEL:]
    mixed = ga * _dot(ya_ref[...], wa_ref[...]) + gb * _dot(yb_ref[...], wb_ref[...])
    o_ref[...] = x_ref[...] + _dot(mixed.astype(BF16), wo_ref[...])


def _merge(x2, ya, yb, gates, wa, wb, wo):
    t = x2.shape[0]
    tm = ROW_TILE
    row = lambda w: pl.BlockSpec((tm, w), lambda i: (i, 0))
    full = lambda a: pl.BlockSpec(a.shape, lambda i: (0, 0), pipeline_mode=pl.Buffered(1))
    return pl.pallas_call(
        _merge_kernel,
        grid=(t // tm,),
        in_specs=[row(D_MODEL), row(SWA_Q_W), row(DIFF_V_W), row(GATE_W), full(wa), full(wb), full(wo)],
        out_specs=row(D_MODEL),
        out_shape=jax.ShapeDtypeStruct((t, D_MODEL), F32),
        compiler_params=pltpu.CompilerParams(dimension_semantics=("parallel",),
                                             vmem_limit_bytes=VMEM_LIMIT),
        name="merge",
    )(x2, ya, yb, gates, wa, wb, wo)


def _conv_mlp_kernel(x_ref, halo_ref, g_ref, wup_ref, cw_ref, cb_ref, wdn_ref, o_ref, u_sc, acc_sc):
    tm = MLP_TILE
    n_chunks = D_FF // FF_CHUNK
    first = pl.program_id(1) == 0

    def normed(x):
        return (x * lax.rsqrt(jnp.mean(x * x, axis=-1, keepdims=True) + EPS) * g_ref[...]).astype(BF16)

    x = x_ref[...]
    h = normed(x)
    h_halo = normed(halo_ref[...])
    keep = jnp.where(first, 0.0, 1.0)

    def columns(c, part):
        return slice(part * D_FF + c * FF_CHUNK, part * D_FF + (c + 1) * FF_CHUNK)

    def up(c):
        for part in range(2):
            cols = columns(c, part)
            u_sc[c % 2, part, 0:SUBLANES, :] = _dot(h_halo, wup_ref[:, cols]) * keep
            u_sc[c % 2, part, SUBLANES:, :] = _dot(h, wup_ref[:, cols])

    def conv(c, part):
        cols = columns(c, part)
        w = cw_ref[:, cols]
        u = u_sc.at[c % 2, part]
        y = (u[SUBLANES - 2:SUBLANES - 2 + tm, :] * w[0:1]
             + u[SUBLANES - 1:SUBLANES - 1 + tm, :] * w[1:2]
             + u[SUBLANES:SUBLANES + tm, :] * w[2:3])
        return y + cb_ref[:, cols]

    up(0)
    for c in range(n_chunks):
        if c + 1 < n_chunks:
            up(c + 1)
        half_g = 0.5 * conv(c, 0)
        act = ((half_g + half_g * jnp.tanh(half_g)) * conv(c, 1)).astype(BF16)
        contrib = _dot(act, wdn_ref[columns(c, 0), :])
        if c + 1 == n_chunks:
            o_ref[...] = x + acc_sc[...] + contrib
        elif c == 0:
            acc_sc[...] = contrib
        else:
            acc_sc[...] += contrib


def _conv_mlp(x3, g_ffn, w_up, conv_w, conv_b, w_down):
    b, s, _ = x3.shape
    tm = MLP_TILE
    halo_blocks = tm // SUBLANES
    full = lambda a: pl.BlockSpec(a.shape, lambda bi, i: (0,) * a.ndim, pipeline_mode=pl.Buffered(1))
    return pl.pallas_call(
        _conv_mlp_kernel,
        grid=(b, s // tm),
        in_specs=[pl.BlockSpec((None, tm, D_MODEL), lambda bi, i: (bi, i, 0)),
                  pl.BlockSpec((None, SUBLANES, D_MODEL),
                               lambda bi, i: (bi, jnp.maximum(i * halo_blocks - 1, 0), 0)),
                  full(g_ffn), full(w_up), full(conv_w), full(conv_b), full(w_down)],
        out_specs=pl.BlockSpec((None, tm, D_MODEL), lambda bi, i: (bi, i, 0)),
        out_shape=jax.ShapeDtypeStruct((b, s, D_MODEL), F32),
        scratch_shapes=[pltpu.VMEM((2, 2, tm + SUBLANES, FF_CHUNK), F32),
                        pltpu.VMEM((tm, D_MODEL), F32)],
        compiler_params=pltpu.CompilerParams(dimension_semantics=("parallel", "parallel"),
                                             vmem_limit_bytes=VMEM_LIMIT),
        name="conv_mlp",
    )(x3, x3, g_ffn, w_up, conv_w, conv_b, w_down)


def _swa_bucket_idx():
    rel = BLOCK + np.arange(BLOCK)[None, :] - np.arange(2 * BLOCK)[:, None]
    return np.where((rel >= 0) & (rel < WINDOW), _t5_bucket_np(rel), -1).astype(np.int32)


def _diff_bucket_idx():
    tq = ATTN_TILE
    rel = tq + np.arange(tq)[None, :] - np.arange(2 * tq)[:, None]
    return np.where(rel >= 0, _t5_bucket_np(rel), -1).astype(np.int32)


def _layer(x, l, rel_bias, swa_tbl, diff_tbl, g_mix, w_in, qn_a, kn_a, sinks, qn_b, kn_b, lam_q1, lam_k1,
           lam_q2, lam_k2, subln_b, w_br_a, w_br_b, w_o, g_ffn, w_up, conv_w, conv_b, w_down):
    b, s, _ = x.shape
    t = b * s
    lam_init = 0.8 - 0.6 * math.exp(-0.3 * l)
    q_scale = HEAD_DIM ** -0.5 * LOG2E
    tile = lambda v, reps, scale=1.0: (jnp.tile(v.astype(F32), reps) * scale).reshape(1, -1)

    x2 = x.reshape(t, D_MODEL)
    qa, kd, vd, qbt, kb, vbt, gates = _in_proj(
        x2, g_mix[l].reshape(1, -1), w_in[l].astype(BF16),
        tile(qn_a[l], SWA_Q_HEADS, q_scale), tile(kn_a[l], SWA_KV_HEADS),
        tile(qn_b[l], 2 * DIFF_HEADS, q_scale), tile(kn_b[l], 2 * DIFF_HEADS))

    sink = jnp.repeat(sinks[l].astype(F32) * LOG2E, BLOCK).reshape(SWA_KV_HEADS, 1, SWA_GROUP * BLOCK)
    ya = _swa_attention(qa.reshape(b, s, -1), kd.reshape(b, s, -1), vd.reshape(b, s, -1), swa_tbl, sink)

    lam_vecs = [v[l].astype(F32).reshape(1, HEAD_DIM) for v in (lam_q1, lam_k1, lam_q2, lam_k2)]
    chunks = s // ATTN_TILE
    yb = _diff_attention(lam_vecs, qbt.reshape(b, chunks, DIFF_QK_W, ATTN_TILE), kb.reshape(b, s, -1),
                         vbt.reshape(b, chunks, DIFF_V_W, ATTN_TILE), diff_tbl,
                         subln_b[l].astype(F32).reshape(-1, 1), lam_init)

    x2 = _merge(x2, ya.reshape(t, -1), yb.reshape(t, -1), gates,
                w_br_a[l].astype(BF16), w_br_b[l].astype(BF16), w_o[l].astype(BF16))
    x3 = _conv_mlp(x2.reshape(b, s, D_MODEL), g_ffn[l].reshape(1, -1), w_up[l].astype(BF16),
                   conv_w[l].astype(F32), conv_b[l].astype(F32).reshape(1, -1), w_down[l].astype(BF16))
    return x3


def kernel(x, rel_bias, g_mix, w_in, qn_a, kn_a, sinks, qn_b, kn_b, lam_q1, lam_k1, lam_q2, lam_k2, subln_b,
           w_br_a, w_br_b, w_o, g_ffn, w_up, conv_w, conv_b, w_down):
    rb = rel_bias.astype(F32)
    swa_tbl = _bias_tables(rb, _swa_bucket_idx(), SWA_Q_HEADS, 0, False)
    diff_tbl = _bias_tables(rb, _diff_bucket_idx(), DIFF_HEADS, SWA_Q_HEADS, True)
    for l in range(g_mix.shape[0]):
        x = _layer(x, l, rb, swa_tbl, diff_tbl, g_mix, w_in, qn_a, kn_a, sinks, qn_b, kn_b, lam_q1, lam_k1,
                   lam_q2, lam_k2, subln_b, w_br_a, w_br_b, w_o, g_ffn, w_up, conv_w, conv_b, w_down)
    return x
```

```python
import functools
import math

import numpy as np
import jax
import jax.numpy as jnp
from jax import lax
from jax.experimental import pallas as pl
from jax.experimental.pallas import tpu as pltpu

D_MODEL = 1024
HEAD_DIM = 64
SWA_Q_HEADS = 8
SWA_KV_HEADS = 2
SWA_GROUP = SWA_Q_HEADS // SWA_KV_HEADS
WINDOW = 128
BLOCK = 128
DIFF_HEADS = 4
DIFF_V_DIM = 2 * HEAD_DIM
N_BUCKETS = 32
MAX_DISTANCE = 128
D_FF = 2816
CONV_WIDTH = 3
EPS = 1e-6

SWA_Q_W = SWA_Q_HEADS * HEAD_DIM
SWA_KV_W = SWA_KV_HEADS * HEAD_DIM
DIFF_QK_W = DIFF_HEADS * 2 * HEAD_DIM
DIFF_V_W = DIFF_HEADS * DIFF_V_DIM
GATE_W = 2 * D_MODEL
IN_WIDTH = SWA_Q_W + 2 * SWA_KV_W + 2 * DIFF_QK_W + DIFF_V_W + GATE_W

LANES = 128
SUBLANES = 8
LOG2E = math.log2(math.e)
NEG_INF = float("-inf")

ROW_TILE = 512
MLP_TILE = 512
ATTN_TILE = 512
FF_CHUNK = 256
FAR_UNROLL = 8
VMEM_LIMIT = 56 * 1024 * 1024

F32 = jnp.float32
BF16 = jnp.bfloat16


def _t5_bucket_np(rel):
    n = np.maximum(rel, 0)
    max_exact = N_BUCKETS // 2
    nf = np.maximum(n, 1).astype(np.float64)
    large = max_exact + (np.log(nf / max_exact) / math.log(MAX_DISTANCE / max_exact)
                         * (N_BUCKETS - max_exact)).astype(np.int32)
    large = np.minimum(large, N_BUCKETS - 1)
    return np.where(n < max_exact, n, large).astype(np.int32)


def _dot(a, b):
    return jnp.dot(a, b, preferred_element_type=F32)


def _dot_nt(a, b):
    return lax.dot_general(a, b, (((1,), (1,)), ((), ())), preferred_element_type=F32)


def _half_lane_mask(shape):
    return lax.broadcasted_iota(jnp.int32, shape, len(shape) - 1) < HEAD_DIM


def _bias_table_kernel(rb_ref, idx_ref, out_ref, *, col0, shift_far):
    h = pl.program_id(0) + col0
    idx = idx_ref[...]
    shift = rb_ref[N_BUCKETS - 1, h] if shift_far else 0.0
    acc = jnp.full(idx.shape, NEG_INF, F32)
    for b in range(N_BUCKETS):
        acc = jnp.where(idx == b, (rb_ref[b, h] - shift) * LOG2E, acc)
    out_ref[...] = acc


def _bias_tables(rel_bias, idx, n_heads, col0, shift_far):
    r, c = idx.shape
    return pl.pallas_call(
        functools.partial(_bias_table_kernel, col0=col0, shift_far=shift_far),
        grid=(n_heads,),
        in_specs=[pl.BlockSpec(memory_space=pltpu.SMEM),
                  pl.BlockSpec((r, c), lambda h: (0, 0))],
        out_specs=pl.BlockSpec((None, r, c), lambda h: (h, 0, 0)),
        out_shape=jax.ShapeDtypeStruct((n_heads, r, c), F32),
        name="bias_tables",
    )(rel_bias, jnp.asarray(idx))


def _pair_rms_norm(z, gain):
    outs = []
    for j in range(z.shape[1] // LANES):
        s = z[:, j * LANES:(j + 1) * LANES]
        left = _half_lane_mask(s.shape)
        sq = s * s
        tot = jnp.sum(sq, axis=-1, keepdims=True)
        lo = jnp.sum(jnp.where(left, sq, 0.0), axis=-1, keepdims=True)
        ms = jnp.where(left, lo, tot - lo) * (1.0 / HEAD_DIM)
        outs.append(s * lax.rsqrt(ms + EPS))
    return jnp.concatenate(outs, axis=-1) * gain


def _dup_halves(z):
    left = _half_lane_mask(z.shape)
    rolled = pltpu.roll(z, HEAD_DIM, axis=1)
    return jnp.concatenate([jnp.where(left, z, rolled), jnp.where(left, rolled, z)], axis=-1)


def _in_proj_kernel(x_ref, g_ref, w_ref, gqa_ref, gka_ref, gqb_ref, gkb_ref,
                    qa_ref, kd_ref, vd_ref, qbt_ref, kb_ref, vbt_ref, gate_ref):
    x = x_ref[...]
    h = (x * lax.rsqrt(jnp.mean(x * x, axis=-1, keepdims=True) + EPS) * g_ref[...]).astype(BF16)

    off = 0

    def proj(width):
        nonlocal off
        z = _dot(h, w_ref[:, off:off + width])
        off += width
        return z

    qa_ref[...] = _pair_rms_norm(proj(SWA_Q_W), gqa_ref[...]).astype(BF16)
    kd_ref[...] = _dup_halves(_pair_rms_norm(proj(SWA_KV_W), gka_ref[...])).astype(BF16)
    vd_ref[...] = _dup_halves(proj(SWA_KV_W)).astype(BF16)
    qbt_ref[...] = _pair_rms_norm(proj(DIFF_QK_W), gqb_ref[...]).T.astype(BF16)
    kb_ref[...] = _pair_rms_norm(proj(DIFF_QK_W), gkb_ref[...]).astype(BF16)
    vbt_ref[...] = proj(DIFF_V_W).T.astype(BF16)
    gate_ref[...] = jax.nn.sigmoid(proj(GATE_W)).astype(BF16)


def _in_proj(x2, g_mix, w_in, gqa, gka, gqb, gkb):
    t = x2.shape[0]
    tm = ATTN_TILE
    row = lambda w: pl.BlockSpec((tm, w), lambda i: (i, 0))
    full = lambda a: pl.BlockSpec(a.shape, lambda i: (0, 0), pipeline_mode=pl.Buffered(1))
    tposed = lambda w: pl.BlockSpec((None, w, tm), lambda i: (i, 0, 0))
    bf = lambda w: jax.ShapeDtypeStruct((t, w), BF16)
    bft = lambda w: jax.ShapeDtypeStruct((t // tm, w, tm), BF16)
    return pl.pallas_call(
        _in_proj_kernel,
        grid=(t // tm,),
        in_specs=[row(D_MODEL), full(g_mix), full(w_in), full(gqa), full(gka), full(gqb), full(gkb)],
        out_specs=[row(SWA_Q_W), row(2 * SWA_KV_W), row(2 * SWA_KV_W), tposed(DIFF_QK_W), row(DIFF_QK_W),
                   tposed(DIFF_V_W), row(GATE_W)],
        out_shape=[bf(SWA_Q_W), bf(2 * SWA_KV_W), bf(2 * SWA_KV_W), bft(DIFF_QK_W), bf(DIFF_QK_W),
                   bft(DIFF_V_W), bf(GATE_W)],
        compiler_params=pltpu.CompilerParams(dimension_semantics=("parallel",),
                                             vmem_limit_bytes=VMEM_LIMIT),
        name="in_proj",
    )(x2, g_mix, w_in, gqa, gka, gqb, gkb)


def _swa_kernel(q_ref, kd_ref, vd_ref, tbl_ref, sink_ref, o_ref):
    i = pl.program_id(1)
    blocks = ATTN_TILE // BLOCK

    def group_block(n, g, kk, vv, key_rows):
        rows = slice(n * BLOCK, (n + 1) * BLOCK)
        stacked = []
        for j in range(2):
            qp = q_ref[rows, (2 * g + j) * LANES:(2 * g + j + 1) * LANES]
            left = _half_lane_mask(qp.shape)
            zero = jnp.zeros_like(qp)
            stacked += [jnp.where(left, qp, zero), jnp.where(left, zero, qp)]
        bias = jnp.concatenate([tbl_ref[SWA_GROUP * g + j, key_rows, :] for j in range(SWA_GROUP)], axis=1)
        s = _dot_nt(kk, jnp.concatenate(stacked, axis=0)) + bias
        sink = sink_ref[g]
        m = jnp.maximum(jnp.max(s, axis=0, keepdims=True), sink)
        p = jnp.exp2(s - m)
        denom = jnp.sum(p, axis=0, keepdims=True) + jnp.exp2(sink - m)
        p = (p * (1.0 / denom)).astype(BF16)
        o = lax.dot_general(p, vv, (((0,), (0,)), ((), ())), preferred_element_type=F32)
        for j in range(2):
            a = o[(2 * j) * BLOCK:(2 * j + 1) * BLOCK]
            b = o[(2 * j + 1) * BLOCK:(2 * j + 2) * BLOCK]
            left = _half_lane_mask(a.shape)
            o_ref[rows, (2 * g + j) * LANES:(2 * g + j + 1) * LANES] = jnp.where(left, a, b).astype(BF16)

    def with_previous(n, start):
        for g in range(SWA_KV_HEADS):
            cols = slice(g * LANES, (g + 1) * LANES)
            group_block(n, g, kd_ref[pl.ds(start, 2 * BLOCK), cols], vd_ref[pl.ds(start, 2 * BLOCK), cols],
                        slice(0, 2 * BLOCK))

    @pl.when(i == 0)
    def _():
        for g in range(SWA_KV_HEADS):
            cols = slice(g * LANES, (g + 1) * LANES)
            group_block(0, g, kd_ref[0:BLOCK, cols], vd_ref[0:BLOCK, cols], slice(BLOCK, 2 * BLOCK))

    @pl.when(i > 0)
    def _():
        with_previous(0, pl.multiple_of(i * ATTN_TILE - BLOCK, BLOCK))

    for n in range(1, blocks):
        with_previous(n, pl.multiple_of(i * ATTN_TILE + (n - 1) * BLOCK, BLOCK))


def _swa_attention(qa, kd, vd, tbl, sink):
    b, s, _ = qa.shape
    tq = ATTN_TILE
    return pl.pallas_call(
        _swa_kernel,
        grid=(b, s // tq),
        in_specs=[pl.BlockSpec((None, tq, SWA_Q_W), lambda bi, i: (bi, i, 0)),
                  pl.BlockSpec((None, s, 2 * SWA_KV_W), lambda bi, i: (bi, 0, 0)),
                  pl.BlockSpec((None, s, 2 * SWA_KV_W), lambda bi, i: (bi, 0, 0)),
                  pl.BlockSpec(tbl.shape, lambda bi, i: (0, 0, 0)),
                  pl.BlockSpec(sink.shape, lambda bi, i: (0, 0, 0))],
        out_specs=pl.BlockSpec((None, tq, SWA_Q_W), lambda bi, i: (bi, i, 0)),
        out_shape=jax.ShapeDtypeStruct((b, s, SWA_Q_W), BF16),
        compiler_params=pltpu.CompilerParams(dimension_semantics=("parallel", "parallel"),
                                             vmem_limit_bytes=VMEM_LIMIT),
        name="swa_attention",
    )(qa, kd, vd, tbl, sink)


def _diff_kernel(lq1_ref, lk1_ref, lq2_ref, lk2_ref, qt_ref, k_ref, vt_ref, tbl_ref, gain_ref, o_ref,
                 m_sc, l_sc, acc_sc, s_sc, smax_sc, *, lam_init):
    i = pl.program_id(2)
    tq = ATTN_TILE
    qt = qt_ref[...]
    top = lax.broadcasted_iota(jnp.int32, qt.shape, 0) < HEAD_DIM
    zero = jnp.zeros_like(qt)
    qts = (jnp.where(top, qt, zero), jnp.where(top, zero, qt))

    def prefetch_scores(j, c):
        kc = k_ref[pl.ds(pl.multiple_of(j * tq, tq), tq), :]
        s = _dot(kc, qts[c])
        s_sc[c] = s
        smax_sc[c] = jnp.max(s, axis=0, keepdims=True)

    def step(j, j_next, bias, first):
        vct = vt_ref[j]
        for c in range(2):
            s = s_sc[c]
            if bias is not None:
                s = s + bias
                m_new = jnp.max(s, axis=0, keepdims=True)
            else:
                m_new = smax_sc[c]
            if not first:
                m_prev = m_sc[c]
                m_new = jnp.maximum(m_prev, m_new)
                alpha = jnp.exp2(m_prev - m_new)
            p = jnp.exp2(s - m_new)
            psum = jnp.sum(p, axis=0, keepdims=True)
            prefetch_scores(j_next, c)
            pv = _dot(vct, p.astype(BF16))
            m_sc[c] = m_new
            if first:
                l_sc[c] = psum
                acc_sc[c] = pv
            else:
                l_sc[c] = alpha * l_sc[c] + psum
                acc_sc[c] = alpha * acc_sc[c] + pv

    def diagonal_step(j_next):
        for c in range(2):
            prefetch_scores(i, c)
        step(i, j_next, tbl_ref[tq:, :], True)

    @pl.when(i == 0)
    def _():
        diagonal_step(0)

    @pl.when(i > 0)
    def _():
        diagonal_step(i - 1)
        step(i - 1, jnp.maximum(i - 2, 0), tbl_ref[:tq, :], False)

    n_far = jnp.maximum(i - 1, 0)
    half = FAR_UNROLL // 2
    n_single = n_far & (half - 1)
    n_half = n_far & half

    def far_steps(j, count):
        for u in range(count):
            step(j - u, jnp.maximum(j - u - 1, 0), None, False)

    def far_single(t, carry):
        far_steps(i - 2 - t, 1)
        return carry

    lax.fori_loop(0, n_single, far_single, 0)

    @pl.when(n_half > 0)
    def _():
        far_steps(i - 2 - n_single, half)

    def far_group(t, carry):
        far_steps(i - 2 - n_single - n_half - FAR_UNROLL * t, FAR_UNROLL)
        return carry

    lax.fori_loop(0, lax.shift_right_logical(n_far, FAR_UNROLL.bit_length() - 1), far_group, 0)

    lam = (jnp.exp(jnp.sum(lq1_ref[...] * lk1_ref[...], axis=-1, keepdims=True))
           - jnp.exp(jnp.sum(lq2_ref[...] * lk2_ref[...], axis=-1, keepdims=True)) + lam_init)
    o = acc_sc[0] / l_sc[0] - lam * (acc_sc[1] / l_sc[1])
    y = o * lax.rsqrt(jnp.mean(o * o, axis=0, keepdims=True) + EPS) * (gain_ref[...] * (1.0 - lam_init))
    o_ref[...] = y.T.astype(BF16)


def _diff_attention(lam_vecs, qbt, kb, vbt, tbl, gain, lam_init):
    b, s, _ = kb.shape
    tq = ATTN_TILE
    n = s // tq
    vec = pl.BlockSpec((1, HEAD_DIM), lambda bi, h, i: (0, 0))
    return pl.pallas_call(
        functools.partial(_diff_kernel, lam_init=lam_init),
        grid=(b, DIFF_HEADS, n),
        in_specs=[vec, vec, vec, vec,
                  pl.BlockSpec((None, None, LANES, tq), lambda bi, h, i: (bi, i, h, 0)),
                  pl.BlockSpec((None, s, LANES), lambda bi, h, i: (bi, 0, h)),
                  pl.BlockSpec((None, n, LANES, tq), lambda bi, h, i: (bi, 0, h, 0)),
                  pl.BlockSpec((None, 2 * tq, tq), lambda bi, h, i: (h, 0, 0)),
                  pl.BlockSpec((DIFF_V_DIM, 1), lambda bi, h, i: (0, 0))],
        out_specs=pl.BlockSpec((None, tq, LANES), lambda bi, h, i: (bi, i, h)),
        out_shape=jax.ShapeDtypeStruct((b, s, DIFF_V_W), BF16),
        scratch_shapes=[pltpu.VMEM((2, 1, tq), F32), pltpu.VMEM((2, 1, tq), F32),
                        pltpu.VMEM((2, DIFF_V_DIM, tq), F32), pltpu.VMEM((2, tq, tq), F32),
                        pltpu.VMEM((2, 1, tq), F32)],
        compiler_params=pltpu.CompilerParams(dimension_semantics=("parallel", "parallel", "arbitrary"),
                                             vmem_limit_bytes=VMEM_LIMIT),
        name="diff_attention",
    )(*lam_vecs, qbt, kb, vbt, tbl, gain)


def _merge_kernel(x_ref, ya_ref, yb_ref, gate_ref, wa_ref, wb_ref, wo_ref, o_ref):
    ga = gate_ref[:, :D_MODEL].astype(F32)
    gb = gate_ref[:, D_MODEL:].astype(F32)
    mixed = ga * _dot(ya_ref[...], wa_ref[...]) + gb * _dot(yb_ref[...], wb_ref[...])
    o_ref[...] = x_ref[...] + _dot(mixed.astype(BF16), wo_ref[...])


def _merge(x2, ya, yb, gates, wa, wb, wo):
    t = x2.shape[0]
    tm = ROW_TILE
    row = lambda w: pl.BlockSpec((tm, w), lambda i: (i, 0))
    full = lambda a: pl.BlockSpec(a.shape, lambda i: (0, 0), pipeline_mode=pl.Buffered(1))
    return pl.pallas_call(
        _merge_kernel,
        grid=(t // tm,),
        in_specs=[row(D_MODEL), row(SWA_Q_W), row(DIFF_V_W), row(GATE_W), full(wa), full(wb), full(wo)],
        out_specs=row(D_MODEL),
        out_shape=jax.ShapeDtypeStruct((t, D_MODEL), F32),
        compiler_params=pltpu.CompilerParams(dimension_semantics=("parallel",),
                                             vmem_limit_bytes=VMEM_LIMIT),
        name="merge",
    )(x2, ya, yb, gates, wa, wb, wo)


def _conv_mlp_kernel(x_ref, halo_ref, g_ref, wup_ref, cw_ref, cb_ref, wdn_ref, o_ref, u_sc, acc_sc):
    tm = MLP_TILE
    n_chunks = D_FF // FF_CHUNK
    first = pl.program_id(1) == 0

    def normed(x):
        return (x * lax.rsqrt(jnp.mean(x * x, axis=-1, keepdims=True) + EPS) * g_ref[...]).astype(BF16)

    x = x_ref[...]
    h = normed(x)
    h_halo = normed(halo_ref[...])
    keep = jnp.where(first, 0.0, 1.0)

    def columns(c, part):
        return slice(part * D_FF + c * FF_CHUNK, part * D_FF + (c + 1) * FF_CHUNK)

    def up(c):
        for part in range(2):
            cols = columns(c, part)
            u_sc[c % 2, part, 0:SUBLANES, :] = _dot(h_halo, wup_ref[:, cols]) * keep
            u_sc[c % 2, part, SUBLANES:, :] = _dot(h, wup_ref[:, cols])

    def conv(c, part):
        cols = columns(c, part)
        w = cw_ref[:, cols]
        u = u_sc.at[c % 2, part]
        y = (u[SUBLANES - 2:SUBLANES - 2 + tm, :] * w[0:1]
             + u[SUBLANES - 1:SUBLANES - 1 + tm, :] * w[1:2]
             + u[SUBLANES:SUBLANES + tm, :] * w[2:3])
        return y + cb_ref[:, cols]

    up(0)
    for c in range(n_chunks):
        if c + 1 < n_chunks:
            up(c + 1)
        half_g = 0.5 * conv(c, 0)
        act = ((half_g + half_g * jnp.tanh(half_g)) * conv(c, 1)).astype(BF16)
        contrib = _dot(act, wdn_ref[columns(c, 0), :])
        if c + 1 == n_chunks:
            o_ref[...] = x + acc_sc[...] + contrib
        elif c == 0:
            acc_sc[...] = contrib
        else:
            acc_sc[...] += contrib


def _conv_mlp(x3, g_ffn, w_up, conv_w, conv_b, w_down):
    b, s, _ = x3.shape
    tm = MLP_TILE
    halo_blocks = tm // SUBLANES
    full = lambda a: pl.BlockSpec(a.shape, lambda bi, i: (0,) * a.ndim, pipeline_mode=pl.Buffered(1))
    return pl.pallas_call(
        _conv_mlp_kernel,
        grid=(b, s // tm),
        in_specs=[pl.BlockSpec((None, tm, D_MODEL), lambda bi, i: (bi, i, 0)),
                  pl.BlockSpec((None, SUBLANES, D_MODEL),
                               lambda bi, i: (bi, jnp.maximum(i * halo_blocks - 1, 0), 0)),
                  full(g_ffn), full(w_up), full(conv_w), full(conv_b), full(w_down)],
        out_specs=pl.BlockSpec((None, tm, D_MODEL), lambda bi, i: (bi, i, 0)),
        out_shape=jax.ShapeDtypeStruct((b, s, D_MODEL), F32),
        scratch_shapes=[pltpu.VMEM((2, 2, tm + SUBLANES, FF_CHUNK), F32),
                        pltpu.VMEM((tm, D_MODEL), F32)],
        compiler_params=pltpu.CompilerParams(dimension_semantics=("parallel", "parallel"),
                                             vmem_limit_bytes=VMEM_LIMIT),
        name="conv_mlp",
    )(x3, x3, g_ffn, w_up, conv_w, conv_b, w_down)


def _swa_bucket_idx():
    rel = BLOCK + np.arange(BLOCK)[None, :] - np.arange(2 * BLOCK)[:, None]
    return np.where((rel >= 0) & (rel < WINDOW), _t5_bucket_np(rel), -1).astype(np.int32)


def _diff_bucket_idx():
    tq = ATTN_TILE
    rel = tq + np.arange(tq)[None, :] - np.arange(2 * tq)[:, None]
    return np.where(rel >= 0, _t5_bucket_np(rel), -1).astype(np.int32)


def _layer(x, l, rel_bias, swa_tbl, diff_tbl, g_mix, w_in, qn_a, kn_a, sinks, qn_b, kn_b, lam_q1, lam_k1,
           lam_q2, lam_k2, subln_b, w_br_a, w_br_b, w_o, g_ffn, w_up, conv_w, conv_b, w_down):
    b, s, _ = x.shape
    t = b * s
    lam_init = 0.8 - 0.6 * math.exp(-0.3 * l)
    q_scale = HEAD_DIM ** -0.5 * LOG2E
    tile = lambda v, reps, scale=1.0: (jnp.tile(v.astype(F32), reps) * scale).reshape(1, -1)

    x2 = x.reshape(t, D_MODEL)
    qa, kd, vd, qbt, kb, vbt, gates = _in_proj(
        x2, g_mix[l].reshape(1, -1), w_in[l].astype(BF16),
        tile(qn_a[l], SWA_Q_HEADS, q_scale), tile(kn_a[l], SWA_KV_HEADS),
        tile(qn_b[l], 2 * DIFF_HEADS, q_scale), tile(kn_b[l], 2 * DIFF_HEADS))

    sink = jnp.repeat(sinks[l].astype(F32) * LOG2E, BLOCK).reshape(SWA_KV_HEADS, 1, SWA_GROUP * BLOCK)
    ya = _swa_attention(qa.reshape(b, s, -1), kd.reshape(b, s, -1), vd.reshape(b, s, -1), swa_tbl, sink)

    lam_vecs = [v[l].astype(F32).reshape(1, HEAD_DIM) for v in (lam_q1, lam_k1, lam_q2, lam_k2)]
    chunks = s // ATTN_TILE
    yb = _diff_attention(lam_vecs, qbt.reshape(b, chunks, DIFF_QK_W, ATTN_TILE), kb.reshape(b, s, -1),
                         vbt.reshape(b, chunks, DIFF_V_W, ATTN_TILE), diff_tbl,
                         subln_b[l].astype(F32).reshape(-1, 1), lam_init)

    x2 = _merge(x2, ya.reshape(t, -1), yb.reshape(t, -1), gates,
                w_br_a[l].astype(BF16), w_br_b[l].astype(BF16), w_o[l].astype(BF16))
    x3 = _conv_mlp(x2.reshape(b, s, D_MODEL), g_ffn[l].reshape(1, -1), w_up[l].astype(BF16),
                   conv_w[l].astype(F32), conv_b[l].astype(F32).reshape(1, -1), w_down[l].astype(BF16))
    return x3


def kernel(x, rel_bias, g_mix, w_in, qn_a, kn_a, sinks, qn_b, kn_b, lam_q1, lam_k1, lam_q2, lam_k2, subln_b,
           w_br_a, w_br_b, w_o, g_ffn, w_up, conv_w, conv_b, w_down):
    rb = rel_bias.astype(F32)
    swa_tbl = _bias_tables(rb, _swa_bucket_idx(), SWA_Q_HEADS, 0, False)
    diff_tbl = _bias_tables(rb, _diff_bucket_idx(), DIFF_HEADS, SWA_Q_HEADS, True)
    for l in range(g_mix.shape[0]):
        x = _layer(x, l, rb, swa_tbl, diff_tbl, g_mix, w_in, qn_a, kn_a, sinks, qn_b, kn_b, lam_q1, lam_k1,
                   lam_q2, lam_k2, subln_b, w_br_a, w_br_b, w_o, g_ffn, w_up, conv_w, conv_b, w_down)
    return x
```

```python
import functools
import math

import numpy as np
import jax
import jax.numpy as jnp
from jax import lax
from jax.experimental import pallas as pl
from jax.experimental.pallas import tpu as pltpu

D_MODEL = 1024
HEAD_DIM = 64
SWA_Q_HEADS = 8
SWA_KV_HEADS = 2
SWA_GROUP = SWA_Q_HEADS // SWA_KV_HEADS
WINDOW = 128
BLOCK = 128
DIFF_HEADS = 4
DIFF_V_DIM = 2 * HEAD_DIM
N_BUCKETS = 32
MAX_DISTANCE = 128
D_FF = 2816
CONV_WIDTH = 3
EPS = 1e-6

SWA_Q_W = SWA_Q_HEADS * HEAD_DIM
SWA_KV_W = SWA_KV_HEADS * HEAD_DIM
DIFF_QK_W = DIFF_HEADS * 2 * HEAD_DIM
DIFF_V_W = DIFF_HEADS * DIFF_V_DIM
GATE_W = 2 * D_MODEL
IN_WIDTH = SWA_Q_W + 2 * SWA_KV_W + 2 * DIFF_QK_W + DIFF_V_W + GATE_W

LANES = 128
SUBLANES = 8
LOG2E = math.log2(math.e)
NEG_INF = float("-inf")

ROW_TILE = 512
MLP_TILE = 512
ATTN_TILE = 512
FF_CHUNK = 256
FAR_UNROLL = 8
VMEM_LIMIT = 56 * 1024 * 1024

F32 = jnp.float32
BF16 = jnp.bfloat16


def _t5_bucket_np(rel):
    n = np.maximum(rel, 0)
    max_exact = N_BUCKETS // 2
    nf = np.maximum(n, 1).astype(np.float64)
    large = max_exact + (np.log(nf / max_exact) / math.log(MAX_DISTANCE / max_exact)
                         * (N_BUCKETS - max_exact)).astype(np.int32)
    large = np.minimum(large, N_BUCKETS - 1)
    return np.where(n < max_exact, n, large).astype(np.int32)


def _dot(a, b):
    return jnp.dot(a, b, preferred_element_type=F32)


def _dot_nt(a, b):
    return lax.dot_general(a, b, (((1,), (1,)), ((), ())), preferred_element_type=F32)


def _half_lane_mask(shape):
    return lax.broadcasted_iota(jnp.int32, shape, len(shape) - 1) < HEAD_DIM


ZERO_BLOCK = -1
MASKED_BLOCK = -2


def _bias_table_kernel(rb_ref, idx_ref, out_ref, *, col0, shift_far, layout):
    h = pl.program_id(0) + col0
    idx = idx_ref[...]
    shift = rb_ref[N_BUCKETS - 1, h] if shift_far else 0.0
    acc = jnp.full(idx.shape, NEG_INF, F32)
    for b in range(N_BUCKETS):
        acc = jnp.where(idx == b, (rb_ref[b, h] - shift) * LOG2E, acc)
    if layout is None:
        out_ref[...] = acc
        return
    for r, row in enumerate(layout):
        for c, src in enumerate(row):
            if src == ZERO_BLOCK:
                blk = jnp.zeros((BLOCK, BLOCK), F32)
            elif src == MASKED_BLOCK:
                blk = jnp.full((BLOCK, BLOCK), NEG_INF, F32)
            else:
                blk = acc[src * BLOCK:(src + 1) * BLOCK]
            out_ref[r * BLOCK:(r + 1) * BLOCK, c * BLOCK:(c + 1) * BLOCK] = blk


def _bias_tables(rel_bias, idx, n_heads, col0, shift_far, layout=None):
    r, c = idx.shape
    out_r, out_c = (r, c) if layout is None else (len(layout) * BLOCK, len(layout[0]) * BLOCK)
    return pl.pallas_call(
        functools.partial(_bias_table_kernel, col0=col0, shift_far=shift_far, layout=layout),
        grid=(n_heads,),
        in_specs=[pl.BlockSpec(memory_space=pltpu.SMEM),
                  pl.BlockSpec((r, c), lambda h: (0, 0))],
        out_specs=pl.BlockSpec((None, out_r, out_c), lambda h: (h, 0, 0)),
        out_shape=jax.ShapeDtypeStruct((n_heads, out_r, out_c), F32),
        name="bias_tables",
    )(rel_bias, jnp.asarray(idx))


def _pair_rms_norm(z, gain):
    outs = []
    for j in range(z.shape[1] // LANES):
        s = z[:, j * LANES:(j + 1) * LANES]
        left = _half_lane_mask(s.shape)
        sq = s * s
        tot = jnp.sum(sq, axis=-1, keepdims=True)
        lo = jnp.sum(jnp.where(left, sq, 0.0), axis=-1, keepdims=True)
        ms = jnp.where(left, lo, tot - lo) * (1.0 / HEAD_DIM)
        outs.append(s * lax.rsqrt(ms + EPS))
    return jnp.concatenate(outs, axis=-1) * gain


def _dup_halves(z):
    left = _half_lane_mask(z.shape)
    rolled = pltpu.roll(z, HEAD_DIM, axis=1)
    return jnp.concatenate([jnp.where(left, z, rolled), jnp.where(left, rolled, z)], axis=-1)


def _in_proj_kernel(x_ref, g_ref, w_ref, gqa_ref, gka_ref, gqb_ref, gkb_ref,
                    qa_ref, kd_ref, vd_ref, qbt_ref, kb_ref, vbt_ref):
    x = x_ref[...]
    h = (x * lax.rsqrt(jnp.mean(x * x, axis=-1, keepdims=True) + EPS) * g_ref[...]).astype(BF16)

    widths = (SWA_Q_W, SWA_KV_W, SWA_KV_W, DIFF_QK_W, DIFF_QK_W, DIFF_V_W)
    starts = dict(zip(("qa", "ka", "va", "qb", "kb", "vb"), np.cumsum((0,) + widths[:-1])))

    def proj(name, width):
        return _dot(h, w_ref[:, starts[name]:starts[name] + width])

    z_kb = proj("kb", DIFF_QK_W)
    z_qb = proj("qb", DIFF_QK_W)
    kb_ref[...] = _pair_rms_norm(z_kb, gkb_ref[...]).astype(BF16)
    z_qa = proj("qa", SWA_Q_W)
    qbt_ref[...] = _pair_rms_norm(z_qb, gqb_ref[...]).astype(BF16).T
    z_ka = proj("ka", SWA_KV_W)
    qa_ref[...] = _pair_rms_norm(z_qa, gqa_ref[...]).astype(BF16)
    z_vb = proj("vb", DIFF_V_W)
    kd_ref[...] = _dup_halves(_pair_rms_norm(z_ka, gka_ref[...])).astype(BF16)
    z_va = proj("va", SWA_KV_W)
    vbt_ref[...] = z_vb.astype(BF16).T
    vd_ref[...] = _dup_halves(z_va).astype(BF16)


def _in_proj(x2, g_mix, w_in, gqa, gka, gqb, gkb):
    t = x2.shape[0]
    tm = ATTN_TILE
    row = lambda w: pl.BlockSpec((tm, w), lambda i: (i, 0))
    full = lambda a: pl.BlockSpec(a.shape, lambda i: (0, 0), pipeline_mode=pl.Buffered(1))
    tposed = lambda w: pl.BlockSpec((None, w, tm), lambda i: (i, 0, 0))
    bf = lambda w: jax.ShapeDtypeStruct((t, w), BF16)
    bft = lambda w: jax.ShapeDtypeStruct((t // tm, w, tm), BF16)
    return pl.pallas_call(
        _in_proj_kernel,
        grid=(t // tm,),
        in_specs=[row(D_MODEL), full(g_mix), full(w_in), full(gqa), full(gka), full(gqb), full(gkb)],
        out_specs=[row(SWA_Q_W), row(2 * SWA_KV_W), row(2 * SWA_KV_W), tposed(DIFF_QK_W), row(DIFF_QK_W),
                   tposed(DIFF_V_W)],
        out_shape=[bf(SWA_Q_W), bf(2 * SWA_KV_W), bf(2 * SWA_KV_W), bft(DIFF_QK_W), bf(DIFF_QK_W),
                   bft(DIFF_V_W)],
        compiler_params=pltpu.CompilerParams(dimension_semantics=("parallel",),
                                             vmem_limit_bytes=VMEM_LIMIT),
        name="in_proj",
    )(x2, g_mix, w_in, gqa, gka, gqb, gkb)


def _swa_kernel(q_ref, kd_ref, vd_ref, tbl_ref, sink_ref, o_ref):
    i = pl.program_id(1)
    blocks = ATTN_TILE // BLOCK

    def scores(n, g, key_rows, table_rows):
        rows = slice(n * BLOCK, (n + 1) * BLOCK)
        stacked = []
        for j in range(2):
            qp = q_ref[rows, (2 * g + j) * LANES:(2 * g + j + 1) * LANES]
            left = _half_lane_mask(qp.shape)
            zero = jnp.zeros_like(qp)
            stacked += [jnp.where(left, qp, zero), jnp.where(left, zero, qp)]
        kk = kd_ref[key_rows, g * LANES:(g + 1) * LANES]
        bias = jnp.concatenate([tbl_ref[SWA_GROUP * g + j, table_rows, :] for j in range(SWA_GROUP)], axis=1)
        return _dot_nt(kk, jnp.concatenate(stacked, axis=0)) + bias

    def finish(n, g, key_rows, s):
        rows = slice(n * BLOCK, (n + 1) * BLOCK)
        sink = sink_ref[g]
        m = jnp.maximum(jnp.max(s, axis=0, keepdims=True), sink)
        p = jnp.exp2(s - m)
        denom = jnp.sum(p, axis=0, keepdims=True) + jnp.exp2(sink - m)
        p = (p * (1.0 / denom)).astype(BF16)
        vv = vd_ref[key_rows, g * LANES:(g + 1) * LANES]
        o = lax.dot_general(p, vv, (((0,), (0,)), ((), ())), preferred_element_type=F32)
        for j in range(2):
            a = o[(2 * j) * BLOCK:(2 * j + 1) * BLOCK]
            b = o[(2 * j + 1) * BLOCK:(2 * j + 2) * BLOCK]
            left = _half_lane_mask(a.shape)
            o_ref[rows, (2 * g + j) * LANES:(2 * g + j + 1) * LANES] = jnp.where(left, a, b).astype(BF16)

    def run(units):
        s_next = scores(*units[0])
        for k, (n, g, key_rows, _) in enumerate(units):
            s = s_next
            if k + 1 < len(units):
                s_next = scores(*units[k + 1])
            finish(n, g, key_rows, s)

    def with_previous(n):
        start = pl.multiple_of(i * ATTN_TILE + (n - 1) * BLOCK, BLOCK)
        return [(n, g, pl.ds(start, 2 * BLOCK), slice(0, 2 * BLOCK)) for g in range(SWA_KV_HEADS)]

    later_blocks = [u for n in range(1, blocks) for u in with_previous(n)]

    @pl.when(i == 0)
    def _():
        run([(0, g, slice(0, BLOCK), slice(BLOCK, 2 * BLOCK)) for g in range(SWA_KV_HEADS)] + later_blocks)

    @pl.when(i > 0)
    def _():
        run(with_previous(0) + later_blocks)


def _swa_attention(qa, kd, vd, tbl, sink):
    b, s, _ = qa.shape
    tq = ATTN_TILE
    return pl.pallas_call(
        _swa_kernel,
        grid=(b, s // tq),
        in_specs=[pl.BlockSpec((None, tq, SWA_Q_W), lambda bi, i: (bi, i, 0)),
                  pl.BlockSpec((None, s, 2 * SWA_KV_W), lambda bi, i: (bi, 0, 0)),
                  pl.BlockSpec((None, s, 2 * SWA_KV_W), lambda bi, i: (bi, 0, 0)),
                  pl.BlockSpec(tbl.shape, lambda bi, i: (0, 0, 0)),
                  pl.BlockSpec(sink.shape, lambda bi, i: (0, 0, 0))],
        out_specs=pl.BlockSpec((None, tq, SWA_Q_W), lambda bi, i: (bi, i, 0)),
        out_shape=jax.ShapeDtypeStruct((b, s, SWA_Q_W), BF16),
        compiler_params=pltpu.CompilerParams(dimension_semantics=("parallel", "parallel"),
                                             vmem_limit_bytes=VMEM_LIMIT),
        name="swa_attention",
    )(qa, kd, vd, tbl, sink)


def _diff_kernel(lq1_ref, lk1_ref, lq2_ref, lk2_ref, qt_ref, k_ref, vt_ref, tbl_ref, gain_ref, o_ref,
                 m_sc, l_sc, acc_sc, s_sc, smax_sc, *, lam_init):
    i = pl.program_id(2)
    tq = ATTN_TILE
    qt = qt_ref[...]
    top = lax.broadcasted_iota(jnp.int32, qt.shape, 0) < HEAD_DIM
    zero = jnp.zeros_like(qt)
    qts = (jnp.where(top, qt, zero), jnp.where(top, zero, qt))

    def prefetch_scores(j, c):
        kc = k_ref[pl.ds(pl.multiple_of(j * tq, tq), tq), :]
        s = _dot(kc, qts[c])
        s_sc[c] = s
        smax_sc[c] = jnp.max(s, axis=0, keepdims=True)

    def step(j, j_next, bias, first):
        vct = vt_ref[j]
        for c in range(2):
            s = s_sc[c]
            if bias is not None:
                s = s + bias
                m_new = jnp.max(s, axis=0, keepdims=True)
            else:
                m_new = smax_sc[c]
            if not first:
                m_prev = m_sc[c]
                m_new = jnp.maximum(m_prev, m_new)
                alpha = jnp.exp2(m_prev - m_new)
            p = jnp.exp2(s - m_new)
            psum = jnp.sum(p, axis=0, keepdims=True)
            prefetch_scores(j_next, c)
            pv = _dot(vct, p.astype(BF16))
            m_sc[c] = m_new
            if first:
                l_sc[c] = psum
                acc_sc[c] = pv
            else:
                l_sc[c] = alpha * l_sc[c] + psum
                acc_sc[c] = alpha * acc_sc[c] + pv

    def diagonal_step(j_next):
        for c in range(2):
            prefetch_scores(i, c)
        step(i, j_next, tbl_ref[tq:, :], True)

    @pl.when(i == 0)
    def _():
        diagonal_step(0)

    @pl.when(i > 0)
    def _():
        diagonal_step(i - 1)
        step(i - 1, jnp.maximum(i - 2, 0), tbl_ref[:tq, :], False)

    n_far = jnp.maximum(i - 1, 0)
    half = FAR_UNROLL // 2
    n_single = n_far & (half - 1)
    n_half = n_far & half

    def far_steps(j, count):
        for u in range(count):
            step(j - u, jnp.maximum(j - u - 1, 0), None, False)

    def far_single(t, carry):
        far_steps(i - 2 - t, 1)
        return carry

    lax.fori_loop(0, n_single, far_single, 0)

    @pl.when(n_half > 0)
    def _():
        far_steps(i - 2 - n_single, half)

    def far_group(t, carry):
        far_steps(i - 2 - n_single - n_half - FAR_UNROLL * t, FAR_UNROLL)
        return carry

    lax.fori_loop(0, lax.shift_right_logical(n_far, FAR_UNROLL.bit_length() - 1), far_group, 0)

    lam = (jnp.exp(jnp.sum(lq1_ref[...] * lk1_ref[...], axis=-1, keepdims=True))
           - jnp.exp(jnp.sum(lq2_ref[...] * lk2_ref[...], axis=-1, keepdims=True)) + lam_init)
    o = acc_sc[0] / l_sc[0] - lam * (acc_sc[1] / l_sc[1])
    y = o * lax.rsqrt(jnp.mean(o * o, axis=0, keepdims=True) + EPS) * (gain_ref[...] * (1.0 - lam_init))
    o_ref[...] = y.T.astype(BF16)


def _diff_attention(lam_vecs, qbt, kb, vbt, tbl, gain, lam_init):
    b, s, _ = kb.shape
    tq = ATTN_TILE
    n = s // tq
    vec = pl.BlockSpec((1, HEAD_DIM), lambda bi, h, i: (0, 0))
    return pl.pallas_call(
        functools.partial(_diff_kernel, lam_init=lam_init),
        grid=(b, DIFF_HEADS, n),
        in_specs=[vec, vec, vec, vec,
                  pl.BlockSpec((None, None, LANES, tq), lambda bi, h, i: (bi, i, h, 0)),
                  pl.BlockSpec((None, s, LANES), lambda bi, h, i: (bi, 0, h)),
                  pl.BlockSpec((None, n, LANES, tq), lambda bi, h, i: (bi, 0, h, 0)),
                  pl.BlockSpec((None, 2 * tq, tq), lambda bi, h, i: (h, 0, 0)),
                  pl.BlockSpec((DIFF_V_DIM, 1), lambda bi, h, i: (0, 0))],
        out_specs=pl.BlockSpec((None, tq, LANES), lambda bi, h, i: (bi, i, h)),
        out_shape=jax.ShapeDtypeStruct((b, s, DIFF_V_W), BF16),
        scratch_shapes=[pltpu.VMEM((2, 1, tq), F32), pltpu.VMEM((2, 1, tq), F32),
                        pltpu.VMEM((2, DIFF_V_DIM, tq), F32), pltpu.VMEM((2, tq, tq), F32),
                        pltpu.VMEM((2, 1, tq), F32)],
        compiler_params=pltpu.CompilerParams(dimension_semantics=("parallel", "parallel", "arbitrary"),
                                             vmem_limit_bytes=VMEM_LIMIT),
        name="diff_attention",
    )(*lam_vecs, qbt, kb, vbt, tbl, gain)


def _merge_kernel(x_ref, ya_ref, yb_ref, g_ref, wg_ref, wa_ref, wb_ref, wo_ref, o_ref):
    x = x_ref[...]
    h = (x * lax.rsqrt(jnp.mean(x * x, axis=-1, keepdims=True) + EPS) * g_ref[...]).astype(BF16)
    ga = jax.nn.sigmoid(_dot(h, wg_ref[:, :D_MODEL]))
    mixed = ga * _dot(ya_ref[...], wa_ref[...])
    gb = jax.nn.sigmoid(_dot(h, wg_ref[:, D_MODEL:]))
    mixed = mixed + gb * _dot(yb_ref[...], wb_ref[...])
    o_ref[...] = x + _dot(mixed.astype(BF16), wo_ref[...])


def _merge(x2, ya, yb, g_mix, w_gate, wa, wb, wo):
    t = x2.shape[0]
    tm = ROW_TILE
    row = lambda w: pl.BlockSpec((tm, w), lambda i: (i, 0))
    full = lambda a: pl.BlockSpec(a.shape, lambda i: (0, 0), pipeline_mode=pl.Buffered(1))
    return pl.pallas_call(
        _merge_kernel,
        grid=(t // tm,),
        in_specs=[row(D_MODEL), row(SWA_Q_W), row(DIFF_V_W), full(g_mix), full(w_gate), full(wa), full(wb),
                  full(wo)],
        out_specs=row(D_MODEL),
        out_shape=jax.ShapeDtypeStruct((t, D_MODEL), F32),
        compiler_params=pltpu.CompilerParams(dimension_semantics=("parallel",),
                                             vmem_limit_bytes=VMEM_LIMIT),
        name="merge",
    )(x2, ya, yb, g_mix, w_gate, wa, wb, wo)


def _conv_mlp_kernel(x_ref, halo_ref, g_ref, wup_ref, cw_ref, cb_ref, wdn_ref, o_ref, u_sc, acc_sc):
    tm = MLP_TILE
    n_chunks = D_FF // FF_CHUNK
    first = pl.program_id(1) == 0

    def normed(x):
        return (x * lax.rsqrt(jnp.mean(x * x, axis=-1, keepdims=True) + EPS) * g_ref[...]).astype(BF16)

    x = x_ref[...]
    h = normed(x)
    h_halo = normed(halo_ref[...])
    keep = jnp.where(first, 0.0, 1.0)

    def columns(c, part):
        return slice(part * D_FF + c * FF_CHUNK, part * D_FF + (c + 1) * FF_CHUNK)

    def up(c):
        for part in range(2):
            cols = columns(c, part)
            u_sc[c % 2, part, 0:SUBLANES, :] = _dot(h_halo, wup_ref[:, cols]) * keep
            u_sc[c % 2, part, SUBLANES:, :] = _dot(h, wup_ref[:, cols])

    def conv(c, part):
        cols = columns(c, part)
        w = cw_ref[:, cols]
        u = u_sc.at[c % 2, part]
        y = (u[SUBLANES - 2:SUBLANES - 2 + tm, :] * w[0:1]
             + u[SUBLANES - 1:SUBLANES - 1 + tm, :] * w[1:2]
             + u[SUBLANES:SUBLANES + tm, :] * w[2:3])
        return y + cb_ref[:, cols]

    up(0)
    for c in range(n_chunks):
        if c + 1 < n_chunks:
            up(c + 1)
        half_g = 0.5 * conv(c, 0)
        act = ((half_g + half_g * jnp.tanh(half_g)) * conv(c, 1)).astype(BF16)
        contrib = _dot(act, wdn_ref[columns(c, 0), :])
        if c + 1 == n_chunks:
            o_ref[...] = x + acc_sc[...] + contrib
        elif c == 0:
            acc_sc[...] = contrib
        else:
            acc_sc[...] += contrib


def _conv_mlp(x3, g_ffn, w_up, conv_w, conv_b, w_down):
    b, s, _ = x3.shape
    tm = MLP_TILE
    halo_blocks = tm // SUBLANES
    full = lambda a: pl.BlockSpec(a.shape, lambda bi, i: (0,) * a.ndim, pipeline_mode=pl.Buffered(1))
    return pl.pallas_call(
        _conv_mlp_kernel,
        grid=(b, s // tm),
        in_specs=[pl.BlockSpec((None, tm, D_MODEL), lambda bi, i: (bi, i, 0)),
                  pl.BlockSpec((None, SUBLANES, D_MODEL),
                               lambda bi, i: (bi, jnp.maximum(i * halo_blocks - 1, 0), 0)),
                  full(g_ffn), full(w_up), full(conv_w), full(conv_b), full(w_down)],
        out_specs=pl.BlockSpec((None, tm, D_MODEL), lambda bi, i: (bi, i, 0)),
        out_shape=jax.ShapeDtypeStruct((b, s, D_MODEL), F32),
        scratch_shapes=[pltpu.VMEM((2, 2, tm + SUBLANES, FF_CHUNK), F32),
                        pltpu.VMEM((tm, D_MODEL), F32)],
        compiler_params=pltpu.CompilerParams(dimension_semantics=("parallel", "parallel"),
                                             vmem_limit_bytes=VMEM_LIMIT),
        name="conv_mlp",
    )(x3, x3, g_ffn, w_up, conv_w, conv_b, w_down)


def _swa_bucket_idx():
    rel = BLOCK + np.arange(BLOCK)[None, :] - np.arange(2 * BLOCK)[:, None]
    return np.where((rel >= 0) & (rel < WINDOW), _t5_bucket_np(rel), -1).astype(np.int32)


def _diff_bucket_idx():
    tq = ATTN_TILE
    rel = tq + np.arange(tq)[None, :] - np.arange(2 * tq)[:, None]
    full = np.where(rel >= 0, _t5_bucket_np(rel), -1).astype(np.int32)
    nq = tq // BLOCK
    blocks = np.concatenate([full[(nq - 1) * BLOCK:nq * BLOCK, :BLOCK], full[nq * BLOCK:(nq + 1) * BLOCK, :BLOCK]])
    layout = []
    for kb in range(2 * nq):
        row = []
        for qb in range(nq):
            d = kb - qb - nq
            row.append(MASKED_BLOCK if d > 0 else 1 if d == 0 else 0 if d == -1 else ZERO_BLOCK)
        layout.append(row)
    far = N_BUCKETS - 1
    tiled = np.block([[np.full((BLOCK, BLOCK), far) if s == ZERO_BLOCK else np.full((BLOCK, BLOCK), -1)
                       if s == MASKED_BLOCK else blocks[s * BLOCK:(s + 1) * BLOCK] for s in row] for row in layout])
    assert (tiled == full).all()
    return blocks, layout


def _layer(x, l, rel_bias, swa_tbl, diff_tbl, g_mix, w_in, qn_a, kn_a, sinks, qn_b, kn_b, lam_q1, lam_k1,
           lam_q2, lam_k2, subln_b, w_br_a, w_br_b, w_o, g_ffn, w_up, conv_w, conv_b, w_down):
    b, s, _ = x.shape
    t = b * s
    lam_init = 0.8 - 0.6 * math.exp(-0.3 * l)
    q_scale = HEAD_DIM ** -0.5 * LOG2E
    tile = lambda v, reps, scale=1.0: (jnp.tile(v.astype(F32), reps) * scale).reshape(1, -1)

    x2 = x.reshape(t, D_MODEL)
    w_in_bf = w_in[l].astype(BF16)
    g_mix_row = g_mix[l].reshape(1, -1)
    qa, kd, vd, qbt, kb, vbt = _in_proj(
        x2, g_mix_row, w_in_bf[:, :IN_WIDTH - GATE_W],
        tile(qn_a[l], SWA_Q_HEADS, q_scale), tile(kn_a[l], SWA_KV_HEADS),
        tile(qn_b[l], 2 * DIFF_HEADS, q_scale), tile(kn_b[l], 2 * DIFF_HEADS))

    sink = jnp.repeat(sinks[l].astype(F32) * LOG2E, BLOCK).reshape(SWA_KV_HEADS, 1, SWA_GROUP * BLOCK)
    ya = _swa_attention(qa.reshape(b, s, -1), kd.reshape(b, s, -1), vd.reshape(b, s, -1), swa_tbl, sink)

    lam_vecs = [v[l].astype(F32).reshape(1, HEAD_DIM) for v in (lam_q1, lam_k1, lam_q2, lam_k2)]
    chunks = s // ATTN_TILE
    yb = _diff_attention(lam_vecs, qbt.reshape(b, chunks, DIFF_QK_W, ATTN_TILE), kb.reshape(b, s, -1),
                         vbt.reshape(b, chunks, DIFF_V_W, ATTN_TILE), diff_tbl,
                         subln_b[l].astype(F32).reshape(-1, 1), lam_init)

    x2 = _merge(x2, ya.reshape(t, -1), yb.reshape(t, -1), g_mix_row, w_in_bf[:, IN_WIDTH - GATE_W:],
                w_br_a[l].astype(BF16), w_br_b[l].astype(BF16), w_o[l].astype(BF16))
    x3 = _conv_mlp(x2.reshape(b, s, D_MODEL), g_ffn[l].reshape(1, -1), w_up[l].astype(BF16),
                   conv_w[l].astype(F32), conv_b[l].astype(F32).reshape(1, -1), w_down[l].astype(BF16))
    return x3


def kernel(x, rel_bias, g_mix, w_in, qn_a, kn_a, sinks, qn_b, kn_b, lam_q1, lam_k1, lam_q2, lam_k2, subln_b,
           w_br_a, w_br_b, w_o, g_ffn, w_up, conv_w, conv_b, w_down):
    rb = rel_bias.astype(F32)
    swa_tbl = _bias_tables(rb, _swa_bucket_idx(), SWA_Q_HEADS, 0, False)
    diff_idx, diff_layout = _diff_bucket_idx()
    diff_tbl = _bias_tables(rb, diff_idx, DIFF_HEADS, SWA_Q_HEADS, True, diff_layout)
    for l in range(g_mix.shape[0]):
        x = _layer(x, l, rb, swa_tbl, diff_tbl, g_mix, w_in, qn_a, kn_a, sinks, qn_b, kn_b, lam_q1, lam_k1,
                   lam_q2, lam_k2, subln_b, w_br_a, w_br_b, w_o, g_ffn, w_up, conv_w, conv_b, w_down)
    return x
```

```python
import functools
import math

import numpy as np
import jax
import jax.numpy as jnp
from jax import lax
from jax.experimental import pallas as pl
from jax.experimental.pallas import tpu as pltpu

D_MODEL = 1024
HEAD_DIM = 64
SWA_Q_HEADS = 8
SWA_KV_HEADS = 2
SWA_GROUP = SWA_Q_HEADS // SWA_KV_HEADS
WINDOW = 128
BLOCK = 128
DIFF_HEADS = 4
DIFF_V_DIM = 2 * HEAD_DIM
N_BUCKETS = 32
MAX_DISTANCE = 128
D_FF = 2816
CONV_WIDTH = 3
EPS = 1e-6

SWA_Q_W = SWA_Q_HEADS * HEAD_DIM
SWA_KV_W = SWA_KV_HEADS * HEAD_DIM
DIFF_QK_W = DIFF_HEADS * 2 * HEAD_DIM
DIFF_V_W = DIFF_HEADS * DIFF_V_DIM
GATE_W = 2 * D_MODEL
IN_WIDTH = SWA_Q_W + 2 * SWA_KV_W + 2 * DIFF_QK_W + DIFF_V_W + GATE_W

LANES = 128
SUBLANES = 8
LOG2E = math.log2(math.e)
NEG_INF = float("-inf")

ROW_TILE = 1024
MLP_TILE = 512
ATTN_TILE = 512
FF_CHUNK = 256
FAR_UNROLL = 8
VMEM_LIMIT = 56 * 1024 * 1024

F32 = jnp.float32
BF16 = jnp.bfloat16


def _t5_bucket_np(rel):
    n = np.maximum(rel, 0)
    max_exact = N_BUCKETS // 2
    nf = np.maximum(n, 1).astype(np.float64)
    large = max_exact + (np.log(nf / max_exact) / math.log(MAX_DISTANCE / max_exact)
                         * (N_BUCKETS - max_exact)).astype(np.int32)
    large = np.minimum(large, N_BUCKETS - 1)
    return np.where(n < max_exact, n, large).astype(np.int32)


def _dot(a, b):
    return jnp.dot(a, b, preferred_element_type=F32)


def _dot_nt(a, b):
    return lax.dot_general(a, b, (((1,), (1,)), ((), ())), preferred_element_type=F32)


def _half_lane_mask(shape):
    return lax.broadcasted_iota(jnp.int32, shape, len(shape) - 1) < HEAD_DIM


ZERO_BLOCK = -1
MASKED_BLOCK = -2


def _bias_table_kernel(rb_ref, idx_ref, out_ref, *, col0, shift_far, layout):
    h = pl.program_id(0) + col0
    idx = idx_ref[...]
    shift = rb_ref[N_BUCKETS - 1, h] if shift_far else 0.0
    acc = jnp.full(idx.shape, NEG_INF, F32)
    for b in range(N_BUCKETS):
        acc = jnp.where(idx == b, (rb_ref[b, h] - shift) * LOG2E, acc)
    if layout is None:
        out_ref[...] = acc
        return
    for r, row in enumerate(layout):
        for c, src in enumerate(row):
            if src == ZERO_BLOCK:
                blk = jnp.zeros((BLOCK, BLOCK), F32)
            elif src == MASKED_BLOCK:
                blk = jnp.full((BLOCK, BLOCK), NEG_INF, F32)
            else:
                blk = acc[src * BLOCK:(src + 1) * BLOCK]
            out_ref[r * BLOCK:(r + 1) * BLOCK, c * BLOCK:(c + 1) * BLOCK] = blk


def _bias_tables(rel_bias, idx, n_heads, col0, shift_far, layout=None):
    r, c = idx.shape
    out_r, out_c = (r, c) if layout is None else (len(layout) * BLOCK, len(layout[0]) * BLOCK)
    return pl.pallas_call(
        functools.partial(_bias_table_kernel, col0=col0, shift_far=shift_far, layout=layout),
        grid=(n_heads,),
        in_specs=[pl.BlockSpec(memory_space=pltpu.SMEM),
                  pl.BlockSpec((r, c), lambda h: (0, 0))],
        out_specs=pl.BlockSpec((None, out_r, out_c), lambda h: (h, 0, 0)),
        out_shape=jax.ShapeDtypeStruct((n_heads, out_r, out_c), F32),
        name="bias_tables",
    )(rel_bias, jnp.asarray(idx))


def _pair_rms_norm(z, gain):
    outs = []
    for j in range(z.shape[1] // LANES):
        s = z[:, j * LANES:(j + 1) * LANES]
        left = _half_lane_mask(s.shape)
        sq = s * s
        tot = jnp.sum(sq, axis=-1, keepdims=True)
        lo = jnp.sum(jnp.where(left, sq, 0.0), axis=-1, keepdims=True)
        ms = jnp.where(left, lo, tot - lo) * (1.0 / HEAD_DIM)
        outs.append(s * lax.rsqrt(ms + EPS))
    return jnp.concatenate(outs, axis=-1) * gain


def _dup_halves(z):
    left = _half_lane_mask(z.shape)
    rolled = pltpu.roll(z, HEAD_DIM, axis=1)
    return jnp.concatenate([jnp.where(left, z, rolled), jnp.where(left, rolled, z)], axis=-1)


def _in_proj_kernel(x_ref, g_ref, w_ref, gqa_ref, gka_ref, gqb_ref, gkb_ref,
                    qa_ref, kd_ref, vd_ref, qbt_ref, kb_ref, vbt_ref):
    x = x_ref[...]
    h = (x * lax.rsqrt(jnp.mean(x * x, axis=-1, keepdims=True) + EPS) * g_ref[...]).astype(BF16)

    widths = (SWA_Q_W, SWA_KV_W, SWA_KV_W, DIFF_QK_W, DIFF_QK_W, DIFF_V_W)
    starts = dict(zip(("qa", "ka", "va", "qb", "kb", "vb"), np.cumsum((0,) + widths[:-1])))

    def proj(name, width):
        return _dot(h, w_ref[:, starts[name]:starts[name] + width])

    z_kb = proj("kb", DIFF_QK_W)
    z_qb = proj("qb", DIFF_QK_W)
    kb_ref[...] = _pair_rms_norm(z_kb, gkb_ref[...]).astype(BF16)
    z_qa = proj("qa", SWA_Q_W)
    qbt_ref[...] = _pair_rms_norm(z_qb, gqb_ref[...]).astype(BF16).T
    z_ka = proj("ka", SWA_KV_W)
    qa_ref[...] = _pair_rms_norm(z_qa, gqa_ref[...]).astype(BF16)
    z_vb = proj("vb", DIFF_V_W)
    kd_ref[...] = _dup_halves(_pair_rms_norm(z_ka, gka_ref[...])).astype(BF16)
    z_va = proj("va", SWA_KV_W)
    vbt_ref[...] = z_vb.astype(BF16).T
    vd_ref[...] = _dup_halves(z_va).astype(BF16)


def _in_proj(x2, g_mix, w_in, gqa, gka, gqb, gkb):
    t = x2.shape[0]
    tm = ATTN_TILE
    row = lambda w: pl.BlockSpec((tm, w), lambda i: (i, 0))
    full = lambda a: pl.BlockSpec(a.shape, lambda i: (0, 0), pipeline_mode=pl.Buffered(1))
    tposed = lambda w: pl.BlockSpec((None, w, tm), lambda i: (i, 0, 0))
    bf = lambda w: jax.ShapeDtypeStruct((t, w), BF16)
    bft = lambda w: jax.ShapeDtypeStruct((t // tm, w, tm), BF16)
    return pl.pallas_call(
        _in_proj_kernel,
        grid=(t // tm,),
        in_specs=[row(D_MODEL), full(g_mix), full(w_in), full(gqa), full(gka), full(gqb), full(gkb)],
        out_specs=[row(SWA_Q_W), row(2 * SWA_KV_W), row(2 * SWA_KV_W), tposed(DIFF_QK_W), row(DIFF_QK_W),
                   tposed(DIFF_V_W)],
        out_shape=[bf(SWA_Q_W), bf(2 * SWA_KV_W), bf(2 * SWA_KV_W), bft(DIFF_QK_W), bf(DIFF_QK_W),
                   bft(DIFF_V_W)],
        compiler_params=pltpu.CompilerParams(dimension_semantics=("parallel",),
                                             vmem_limit_bytes=VMEM_LIMIT),
        name="in_proj",
    )(x2, g_mix, w_in, gqa, gka, gqb, gkb)


def _swa_kernel(q_ref, kd_ref, vd_ref, tbl_ref, sink_ref, o_ref):
    i = pl.program_id(1)
    blocks = ATTN_TILE // BLOCK

    def scores(n, g, key_rows, table_rows):
        rows = slice(n * BLOCK, (n + 1) * BLOCK)
        stacked = []
        for j in range(2):
            qp = q_ref[rows, (2 * g + j) * LANES:(2 * g + j + 1) * LANES]
            left = _half_lane_mask(qp.shape)
            zero = jnp.zeros_like(qp)
            stacked += [jnp.where(left, qp, zero), jnp.where(left, zero, qp)]
        kk = kd_ref[key_rows, g * LANES:(g + 1) * LANES]
        bias = jnp.concatenate([tbl_ref[SWA_GROUP * g + j, table_rows, :] for j in range(SWA_GROUP)], axis=1)
        return _dot_nt(kk, jnp.concatenate(stacked, axis=0)) + bias

    def finish(n, g, key_rows, s):
        rows = slice(n * BLOCK, (n + 1) * BLOCK)
        sink = sink_ref[g]
        m = jnp.maximum(jnp.max(s, axis=0, keepdims=True), sink)
        p = jnp.exp2(s - m)
        denom = jnp.sum(p, axis=0, keepdims=True) + jnp.exp2(sink - m)
        p = (p * (1.0 / denom)).astype(BF16)
        vv = vd_ref[key_rows, g * LANES:(g + 1) * LANES]
        o = lax.dot_general(p, vv, (((0,), (0,)), ((), ())), preferred_element_type=F32)
        for j in range(2):
            a = o[(2 * j) * BLOCK:(2 * j + 1) * BLOCK]
            b = o[(2 * j + 1) * BLOCK:(2 * j + 2) * BLOCK]
            left = _half_lane_mask(a.shape)
            o_ref[rows, (2 * g + j) * LANES:(2 * g + j + 1) * LANES] = jnp.where(left, a, b).astype(BF16)

    def run(units):
        s_next = scores(*units[0])
        for k, (n, g, key_rows, _) in enumerate(units):
            s = s_next
            if k + 1 < len(units):
                s_next = scores(*units[k + 1])
            finish(n, g, key_rows, s)

    def with_previous(n):
        start = pl.multiple_of(i * ATTN_TILE + (n - 1) * BLOCK, BLOCK)
        return [(n, g, pl.ds(start, 2 * BLOCK), slice(0, 2 * BLOCK)) for g in range(SWA_KV_HEADS)]

    later_blocks = [u for n in range(1, blocks) for u in with_previous(n)]

    @pl.when(i == 0)
    def _():
        run([(0, g, slice(0, BLOCK), slice(BLOCK, 2 * BLOCK)) for g in range(SWA_KV_HEADS)] + later_blocks)

    @pl.when(i > 0)
    def _():
        run(with_previous(0) + later_blocks)


def _swa_attention(qa, kd, vd, tbl, sink):
    b, s, _ = qa.shape
    tq = ATTN_TILE
    return pl.pallas_call(
        _swa_kernel,
        grid=(b, s // tq),
        in_specs=[pl.BlockSpec((None, tq, SWA_Q_W), lambda bi, i: (bi, i, 0)),
                  pl.BlockSpec((None, s, 2 * SWA_KV_W), lambda bi, i: (bi, 0, 0)),
                  pl.BlockSpec((None, s, 2 * SWA_KV_W), lambda bi, i: (bi, 0, 0)),
                  pl.BlockSpec(tbl.shape, lambda bi, i: (0, 0, 0)),
                  pl.BlockSpec(sink.shape, lambda bi, i: (0, 0, 0))],
        out_specs=pl.BlockSpec((None, tq, SWA_Q_W), lambda bi, i: (bi, i, 0)),
        out_shape=jax.ShapeDtypeStruct((b, s, SWA_Q_W), BF16),
        compiler_params=pltpu.CompilerParams(dimension_semantics=("parallel", "parallel"),
                                             vmem_limit_bytes=VMEM_LIMIT),
        name="swa_attention",
    )(qa, kd, vd, tbl, sink)


def _diff_kernel(lq1_ref, lk1_ref, lq2_ref, lk2_ref, qt_ref, k_ref, vt_ref, tbl_ref, gain_ref, o_ref,
                 m_sc, l_sc, acc_sc, s_sc, smax_sc, *, lam_init):
    i = pl.program_id(2)
    tq = ATTN_TILE
    qt = qt_ref[...]
    top = lax.broadcasted_iota(jnp.int32, qt.shape, 0) < HEAD_DIM
    zero = jnp.zeros_like(qt)
    qts = (jnp.where(top, qt, zero), jnp.where(top, zero, qt))

    def prefetch_scores(j, c):
        kc = k_ref[pl.ds(pl.multiple_of(j * tq, tq), tq), :]
        s = _dot(kc, qts[c])
        s_sc[c] = s
        smax_sc[c] = jnp.max(s, axis=0, keepdims=True)

    def step(j, j_next, bias, first):
        vct = vt_ref[j]
        for c in range(2):
            s = s_sc[c]
            if bias is not None:
                s = s + bias
                m_new = jnp.max(s, axis=0, keepdims=True)
            else:
                m_new = smax_sc[c]
            if not first:
                m_prev = m_sc[c]
                m_new = jnp.maximum(m_prev, m_new)
                alpha = jnp.exp2(m_prev - m_new)
            p = jnp.exp2(s - m_new)
            psum = jnp.sum(p, axis=0, keepdims=True)
            prefetch_scores(j_next, c)
            pv = _dot(vct, p.astype(BF16))
            m_sc[c] = m_new
            if first:
                l_sc[c] = psum
                acc_sc[c] = pv
            else:
                l_sc[c] = alpha * l_sc[c] + psum
                acc_sc[c] = alpha * acc_sc[c] + pv

    def diagonal_step(j_next):
        for c in range(2):
            prefetch_scores(i, c)
        step(i, j_next, tbl_ref[tq:, :], True)

    @pl.when(i == 0)
    def _():
        diagonal_step(0)

    @pl.when(i > 0)
    def _():
        diagonal_step(i - 1)
        step(i - 1, jnp.maximum(i - 2, 0), tbl_ref[:tq, :], False)

    n_far = jnp.maximum(i - 1, 0)
    half = FAR_UNROLL // 2
    n_single = n_far & (half - 1)
    n_half = n_far & half

    def far_steps(j, count):
        for u in range(count):
            step(j - u, jnp.maximum(j - u - 1, 0), None, False)

    def far_single(t, carry):
        far_steps(i - 2 - t, 1)
        return carry

    lax.fori_loop(0, n_single, far_single, 0)

    @pl.when(n_half > 0)
    def _():
        far_steps(i - 2 - n_single, half)

    def far_group(t, carry):
        far_steps(i - 2 - n_single - n_half - FAR_UNROLL * t, FAR_UNROLL)
        return carry

    lax.fori_loop(0, lax.shift_right_logical(n_far, FAR_UNROLL.bit_length() - 1), far_group, 0)

    lam = (jnp.exp(jnp.sum(lq1_ref[...] * lk1_ref[...], axis=-1, keepdims=True))
           - jnp.exp(jnp.sum(lq2_ref[...] * lk2_ref[...], axis=-1, keepdims=True)) + lam_init)
    o = acc_sc[0] / l_sc[0] - lam * (acc_sc[1] / l_sc[1])
    y = o * lax.rsqrt(jnp.mean(o * o, axis=0, keepdims=True) + EPS) * (gain_ref[...] * (1.0 - lam_init))
    o_ref[...] = y.astype(BF16).T


def _diff_attention(lam_vecs, qbt, kb, vbt, tbl, gain, lam_init):
    b, s, _ = kb.shape
    tq = ATTN_TILE
    n = s // tq
    vec = pl.BlockSpec((1, HEAD_DIM), lambda bi, h, i: (0, 0))
    return pl.pallas_call(
        functools.partial(_diff_kernel, lam_init=lam_init),
        grid=(b, DIFF_HEADS, n),
        in_specs=[vec, vec, vec, vec,
                  pl.BlockSpec((None, None, LANES, tq), lambda bi, h, i: (bi, i, h, 0)),
                  pl.BlockSpec((None, s, LANES), lambda bi, h, i: (bi, 0, h)),
                  pl.BlockSpec((None, n, LANES, tq), lambda bi, h, i: (bi, 0, h, 0)),
                  pl.BlockSpec((None, 2 * tq, tq), lambda bi, h, i: (h, 0, 0)),
                  pl.BlockSpec((DIFF_V_DIM, 1), lambda bi, h, i: (0, 0))],
        out_specs=pl.BlockSpec((None, tq, LANES), lambda bi, h, i: (bi, i, h)),
        out_shape=jax.ShapeDtypeStruct((b, s, DIFF_V_W), BF16),
        scratch_shapes=[pltpu.VMEM((2, 1, tq), F32), pltpu.VMEM((2, 1, tq), F32),
                        pltpu.VMEM((2, DIFF_V_DIM, tq), F32), pltpu.VMEM((2, tq, tq), F32),
                        pltpu.VMEM((2, 1, tq), F32)],
        compiler_params=pltpu.CompilerParams(dimension_semantics=("parallel", "parallel", "arbitrary"),
                                             vmem_limit_bytes=VMEM_LIMIT),
        name="diff_attention",
    )(*lam_vecs, qbt, kb, vbt, tbl, gain)


def _merge_kernel(x_ref, ya_ref, yb_ref, g_ref, wg_ref, wa_ref, wb_ref, wo_ref, o_ref):
    x = x_ref[...]
    h = (x * lax.rsqrt(jnp.mean(x * x, axis=-1, keepdims=True) + EPS) * g_ref[...]).astype(BF16)
    ga = jax.nn.sigmoid(_dot(h, wg_ref[:, :D_MODEL]))
    mixed = ga * _dot(ya_ref[...], wa_ref[...])
    gb = jax.nn.sigmoid(_dot(h, wg_ref[:, D_MODEL:]))
    mixed = mixed + gb * _dot(yb_ref[...], wb_ref[...])
    o_ref[...] = x + _dot(mixed.astype(BF16), wo_ref[...])


def _merge(x2, ya, yb, g_mix, w_gate, wa, wb, wo):
    t = x2.shape[0]
    tm = ROW_TILE
    row = lambda w: pl.BlockSpec((tm, w), lambda i: (i, 0))
    full = lambda a: pl.BlockSpec(a.shape, lambda i: (0, 0), pipeline_mode=pl.Buffered(1))
    return pl.pallas_call(
        _merge_kernel,
        grid=(t // tm,),
        in_specs=[row(D_MODEL), row(SWA_Q_W), row(DIFF_V_W), full(g_mix), full(w_gate), full(wa), full(wb),
                  full(wo)],
        out_specs=row(D_MODEL),
        out_shape=jax.ShapeDtypeStruct((t, D_MODEL), F32),
        compiler_params=pltpu.CompilerParams(dimension_semantics=("parallel",),
                                             vmem_limit_bytes=VMEM_LIMIT),
        name="merge",
    )(x2, ya, yb, g_mix, w_gate, wa, wb, wo)


def _conv_mlp_kernel(x_ref, halo_ref, g_ref, wup_ref, cw_ref, cb_ref, wdn_ref, o_ref, u_sc, acc_sc):
    tm = MLP_TILE
    n_chunks = D_FF // FF_CHUNK
    first = pl.program_id(1) == 0

    def normed(x):
        return (x * lax.rsqrt(jnp.mean(x * x, axis=-1, keepdims=True) + EPS) * g_ref[...]).astype(BF16)

    x = x_ref[...]
    h = normed(x)
    h_halo = normed(halo_ref[...])
    keep = jnp.where(first, 0.0, 1.0)

    def columns(c, part):
        return slice(part * D_FF + c * FF_CHUNK, part * D_FF + (c + 1) * FF_CHUNK)

    def up(c):
        for part in range(2):
            cols = columns(c, part)
            u_sc[c % 2, part, 0:SUBLANES, :] = _dot(h_halo, wup_ref[:, cols]) * keep
            u_sc[c % 2, part, SUBLANES:, :] = _dot(h, wup_ref[:, cols])

    def conv(c, part):
        cols = columns(c, part)
        w = cw_ref[:, cols]
        u = u_sc.at[c % 2, part]
        y = (u[SUBLANES - 2:SUBLANES - 2 + tm, :] * w[0:1]
             + u[SUBLANES - 1:SUBLANES - 1 + tm, :] * w[1:2]
             + u[SUBLANES:SUBLANES + tm, :] * w[2:3])
        return y + cb_ref[:, cols]

    up(0)
    for c in range(n_chunks):
        if c + 1 < n_chunks:
            up(c + 1)
        half_g = 0.5 * conv(c, 0)
        act = ((half_g + half_g * jnp.tanh(half_g)) * conv(c, 1)).astype(BF16)
        contrib = _dot(act, wdn_ref[columns(c, 0), :])
        if c + 1 == n_chunks:
            o_ref[...] = x + acc_sc[...] + contrib
        elif c == 0:
            acc_sc[...] = contrib
        else:
            acc_sc[...] += contrib


def _conv_mlp(x3, g_ffn, w_up, conv_w, conv_b, w_down):
    b, s, _ = x3.shape
    tm = MLP_TILE
    halo_blocks = tm // SUBLANES
    full = lambda a: pl.BlockSpec(a.shape, lambda bi, i: (0,) * a.ndim, pipeline_mode=pl.Buffered(1))
    return pl.pallas_call(
        _conv_mlp_kernel,
        grid=(b, s // tm),
        in_specs=[pl.BlockSpec((None, tm, D_MODEL), lambda bi, i: (bi, i, 0)),
                  pl.BlockSpec((None, SUBLANES, D_MODEL),
                               lambda bi, i: (bi, jnp.maximum(i * halo_blocks - 1, 0), 0)),
                  full(g_ffn), full(w_up), full(conv_w), full(conv_b), full(w_down)],
        out_specs=pl.BlockSpec((None, tm, D_MODEL), lambda bi, i: (bi, i, 0)),
        out_shape=jax.ShapeDtypeStruct((b, s, D_MODEL), F32),
        scratch_shapes=[pltpu.VMEM((2, 2, tm + SUBLANES, FF_CHUNK), F32),
                        pltpu.VMEM((tm, D_MODEL), F32)],
        compiler_params=pltpu.CompilerParams(dimension_semantics=("parallel", "parallel"),
                                             vmem_limit_bytes=VMEM_LIMIT),
        name="conv_mlp",
    )(x3, x3, g_ffn, w_up, conv_w, conv_b, w_down)


def _swa_bucket_idx():
    rel = BLOCK + np.arange(BLOCK)[None, :] - np.arange(2 * BLOCK)[:, None]
    return np.where((rel >= 0) & (rel < WINDOW), _t5_bucket_np(rel), -1).astype(np.int32)


def _diff_bucket_idx():
    tq = ATTN_TILE
    rel = tq + np.arange(tq)[None, :] - np.arange(2 * tq)[:, None]
    full = np.where(rel >= 0, _t5_bucket_np(rel), -1).astype(np.int32)
    nq = tq // BLOCK
    blocks = np.concatenate([full[(nq - 1) * BLOCK:nq * BLOCK, :BLOCK], full[nq * BLOCK:(nq + 1) * BLOCK, :BLOCK]])
    layout = []
    for kb in range(2 * nq):
        row = []
        for qb in range(nq):
            d = kb - qb - nq
            row.append(MASKED_BLOCK if d > 0 else 1 if d == 0 else 0 if d == -1 else ZERO_BLOCK)
        layout.append(row)
    far = N_BUCKETS - 1
    tiled = np.block([[np.full((BLOCK, BLOCK), far) if s == ZERO_BLOCK else np.full((BLOCK, BLOCK), -1)
                       if s == MASKED_BLOCK else blocks[s * BLOCK:(s + 1) * BLOCK] for s in row] for row in layout])
    assert (tiled == full).all()
    return blocks, layout


def _layer(x, l, swa_tbl, diff_tbl, g_mix, w_in, qn_a, kn_a, sinks, qn_b, kn_b, lam_q1, lam_k1,
           lam_q2, lam_k2, subln_b, w_br_a, w_br_b, w_o, g_ffn, w_up, conv_w, conv_b, w_down):
    b, s, _ = x.shape
    t = b * s
    lam_init = 0.8 - 0.6 * math.exp(-0.3 * l)
    q_scale = HEAD_DIM ** -0.5 * LOG2E
    tile = lambda v, reps, scale=1.0: (jnp.tile(v.astype(F32), reps) * scale).reshape(1, -1)

    x2 = x.reshape(t, D_MODEL)
    w_in_bf = w_in[l].astype(BF16)
    g_mix_row = g_mix[l].reshape(1, -1)
    qa, kd, vd, qbt, kb, vbt = _in_proj(
        x2, g_mix_row, w_in_bf[:, :IN_WIDTH - GATE_W],
        tile(qn_a[l], SWA_Q_HEADS, q_scale), tile(kn_a[l], SWA_KV_HEADS),
        tile(qn_b[l], 2 * DIFF_HEADS, q_scale), tile(kn_b[l], 2 * DIFF_HEADS))

    sink = jnp.repeat(sinks[l].astype(F32) * LOG2E, BLOCK).reshape(SWA_KV_HEADS, 1, SWA_GROUP * BLOCK)
    ya = _swa_attention(qa.reshape(b, s, -1), kd.reshape(b, s, -1), vd.reshape(b, s, -1), swa_tbl, sink)

    lam_vecs = [v[l].astype(F32).reshape(1, HEAD_DIM) for v in (lam_q1, lam_k1, lam_q2, lam_k2)]
    chunks = s // ATTN_TILE
    yb = _diff_attention(lam_vecs, qbt.reshape(b, chunks, DIFF_QK_W, ATTN_TILE), kb.reshape(b, s, -1),
                         vbt.reshape(b, chunks, DIFF_V_W, ATTN_TILE), diff_tbl,
                         subln_b[l].astype(F32).reshape(-1, 1), lam_init)

    x2 = _merge(x2, ya.reshape(t, -1), yb.reshape(t, -1), g_mix_row, w_in_bf[:, IN_WIDTH - GATE_W:],
                w_br_a[l].astype(BF16), w_br_b[l].astype(BF16), w_o[l].astype(BF16))
    x3 = _conv_mlp(x2.reshape(b, s, D_MODEL), g_ffn[l].reshape(1, -1), w_up[l].astype(BF16),
                   conv_w[l].astype(F32), conv_b[l].astype(F32).reshape(1, -1), w_down[l].astype(BF16))
    return x3


def kernel(x, rel_bias, g_mix, w_in, qn_a, kn_a, sinks, qn_b, kn_b, lam_q1, lam_k1, lam_q2, lam_k2, subln_b,
           w_br_a, w_br_b, w_o, g_ffn, w_up, conv_w, conv_b, w_down):
    rb = rel_bias.astype(F32)
    swa_tbl = _bias_tables(rb, _swa_bucket_idx(), SWA_Q_HEADS, 0, False)
    diff_idx, diff_layout = _diff_bucket_idx()
    diff_tbl = _bias_tables(rb, diff_idx, DIFF_HEADS, SWA_Q_HEADS, True, diff_layout)
    for l in range(g_mix.shape[0]):
        x = _layer(x, l, swa_tbl, diff_tbl, g_mix, w_in, qn_a, kn_a, sinks, qn_b, kn_b, lam_q1, lam_k1,
                   lam_q2, lam_k2, subln_b, w_br_a, w_br_b, w_o, g_ffn, w_up, conv_w, conv_b, w_down)
    return x
```

```python
import functools
import math

import numpy as np
import jax
import jax.numpy as jnp
from jax import lax
from jax.experimental import pallas as pl
from jax.experimental.pallas import tpu as pltpu

D_MODEL = 1024
HEAD_DIM = 64
SWA_Q_HEADS = 8
SWA_KV_HEADS = 2
SWA_GROUP = SWA_Q_HEADS // SWA_KV_HEADS
WINDOW = 128
BLOCK = 128
DIFF_HEADS = 4
DIFF_V_DIM = 2 * HEAD_DIM
N_BUCKETS = 32
MAX_DISTANCE = 128
D_FF = 2816
CONV_WIDTH = 3
EPS = 1e-6

SWA_Q_W = SWA_Q_HEADS * HEAD_DIM
SWA_KV_W = SWA_KV_HEADS * HEAD_DIM
DIFF_QK_W = DIFF_HEADS * 2 * HEAD_DIM
DIFF_V_W = DIFF_HEADS * DIFF_V_DIM
GATE_W = 2 * D_MODEL
IN_WIDTH = SWA_Q_W + 2 * SWA_KV_W + 2 * DIFF_QK_W + DIFF_V_W + GATE_W

LANES = 128
SUBLANES = 8
LOG2E = math.log2(math.e)
NEG_INF = float("-inf")

ROW_TILE = 1024
MLP_TILE = 512
SWA_TILE = 1024
ATTN_TILE = 512
FF_CHUNK = 256
FAR_UNROLL = 8
VMEM_LIMIT = 56 * 1024 * 1024

F32 = jnp.float32
BF16 = jnp.bfloat16


def _t5_bucket_np(rel):
    n = np.maximum(rel, 0)
    max_exact = N_BUCKETS // 2
    nf = np.maximum(n, 1).astype(np.float64)
    large = max_exact + (np.log(nf / max_exact) / math.log(MAX_DISTANCE / max_exact)
                         * (N_BUCKETS - max_exact)).astype(np.int32)
    large = np.minimum(large, N_BUCKETS - 1)
    return np.where(n < max_exact, n, large).astype(np.int32)


def _dot(a, b):
    return jnp.dot(a, b, preferred_element_type=F32)


def _dot_nt(a, b):
    return lax.dot_general(a, b, (((1,), (1,)), ((), ())), preferred_element_type=F32)


def _half_lane_mask(shape):
    return lax.broadcasted_iota(jnp.int32, shape, len(shape) - 1) < HEAD_DIM


ZERO_BLOCK = -1
MASKED_BLOCK = -2


def _bias_table_kernel(rb_ref, idx_ref, out_ref, *, col0, shift_far, layout):
    h = pl.program_id(0) + col0
    idx = idx_ref[...]
    shift = rb_ref[N_BUCKETS - 1, h] if shift_far else 0.0
    acc = jnp.full(idx.shape, NEG_INF, F32)
    for b in range(N_BUCKETS):
        acc = jnp.where(idx == b, (rb_ref[b, h] - shift) * LOG2E, acc)
    if layout is None:
        out_ref[...] = acc
        return
    for r, row in enumerate(layout):
        for c, src in enumerate(row):
            if src == ZERO_BLOCK:
                blk = jnp.zeros((BLOCK, BLOCK), F32)
            elif src == MASKED_BLOCK:
                blk = jnp.full((BLOCK, BLOCK), NEG_INF, F32)
            else:
                blk = acc[src * BLOCK:(src + 1) * BLOCK]
            out_ref[r * BLOCK:(r + 1) * BLOCK, c * BLOCK:(c + 1) * BLOCK] = blk


def _bias_tables(rel_bias, idx, n_heads, col0, shift_far, layout=None):
    r, c = idx.shape
    out_r, out_c = (r, c) if layout is None else (len(layout) * BLOCK, len(layout[0]) * BLOCK)
    return pl.pallas_call(
        functools.partial(_bias_table_kernel, col0=col0, shift_far=shift_far, layout=layout),
        grid=(n_heads,),
        in_specs=[pl.BlockSpec(memory_space=pltpu.SMEM),
                  pl.BlockSpec((r, c), lambda h: (0, 0))],
        out_specs=pl.BlockSpec((None, out_r, out_c), lambda h: (h, 0, 0)),
        out_shape=jax.ShapeDtypeStruct((n_heads, out_r, out_c), F32),
        name="bias_tables",
    )(rel_bias, jnp.asarray(idx))


def _pair_rms_norm(z, gain):
    outs = []
    for j in range(z.shape[1] // LANES):
        s = z[:, j * LANES:(j + 1) * LANES]
        left = _half_lane_mask(s.shape)
        sq = s * s
        tot = jnp.sum(sq, axis=-1, keepdims=True)
        lo = jnp.sum(jnp.where(left, sq, 0.0), axis=-1, keepdims=True)
        ms = jnp.where(left, lo, tot - lo) * (1.0 / HEAD_DIM)
        outs.append(s * lax.rsqrt(ms + EPS))
    return jnp.concatenate(outs, axis=-1) * gain


def _dup_halves(z):
    left = _half_lane_mask(z.shape)
    rolled = pltpu.roll(z, HEAD_DIM, axis=1)
    return jnp.concatenate([jnp.where(left, z, rolled), jnp.where(left, rolled, z)], axis=-1)


def _in_proj_kernel(x_ref, g_ref, w_ref, gqa_ref, gka_ref, gqb_ref, gkb_ref,
                    qa_ref, kd_ref, vd_ref, qbt_ref, kb_ref, vbt_ref):
    x = x_ref[...]
    h = (x * lax.rsqrt(jnp.mean(x * x, axis=-1, keepdims=True) + EPS) * g_ref[...]).astype(BF16)

    widths = (SWA_Q_W, SWA_KV_W, SWA_KV_W, DIFF_QK_W, DIFF_QK_W, DIFF_V_W)
    starts = dict(zip(("qa", "ka", "va", "qb", "kb", "vb"), np.cumsum((0,) + widths[:-1])))

    def proj(name, width):
        return _dot(h, w_ref[:, starts[name]:starts[name] + width])

    z_kb = proj("kb", DIFF_QK_W)
    z_qb = proj("qb", DIFF_QK_W)
    kb_ref[...] = _pair_rms_norm(z_kb, gkb_ref[...]).astype(BF16)
    z_qa = proj("qa", SWA_Q_W)
    qbt_ref[...] = _pair_rms_norm(z_qb, gqb_ref[...]).astype(BF16).T
    z_ka = proj("ka", SWA_KV_W)
    qa_ref[...] = _pair_rms_norm(z_qa, gqa_ref[...]).astype(BF16)
    z_vb = proj("vb", DIFF_V_W)
    kd_ref[...] = _dup_halves(_pair_rms_norm(z_ka, gka_ref[...])).astype(BF16)
    z_va = proj("va", SWA_KV_W)
    vbt_ref[...] = z_vb.astype(BF16).T
    vd_ref[...] = _dup_halves(z_va).astype(BF16)


def _in_proj(x2, g_mix, w_in, gqa, gka, gqb, gkb):
    t = x2.shape[0]
    tm = ATTN_TILE
    row = lambda w: pl.BlockSpec((tm, w), lambda i: (i, 0))
    full = lambda a: pl.BlockSpec(a.shape, lambda i: (0, 0), pipeline_mode=pl.Buffered(1))
    tposed = lambda w: pl.BlockSpec((None, w, tm), lambda i: (i, 0, 0))
    bf = lambda w: jax.ShapeDtypeStruct((t, w), BF16)
    bft = lambda w: jax.ShapeDtypeStruct((t // tm, w, tm), BF16)
    return pl.pallas_call(
        _in_proj_kernel,
        grid=(t // tm,),
        in_specs=[row(D_MODEL), full(g_mix), full(w_in), full(gqa), full(gka), full(gqb), full(gkb)],
        out_specs=[row(SWA_Q_W), row(2 * SWA_KV_W), row(2 * SWA_KV_W), tposed(DIFF_QK_W), row(DIFF_QK_W),
                   tposed(DIFF_V_W)],
        out_shape=[bf(SWA_Q_W), bf(2 * SWA_KV_W), bf(2 * SWA_KV_W), bft(DIFF_QK_W), bf(DIFF_QK_W),
                   bft(DIFF_V_W)],
        compiler_params=pltpu.CompilerParams(dimension_semantics=("parallel",),
                                             vmem_limit_bytes=VMEM_LIMIT),
        name="in_proj",
    )(x2, g_mix, w_in, gqa, gka, gqb, gkb)


def _swa_kernel(q_ref, kd_ref, vd_ref, tbl_ref, sink_ref, o_ref):
    i = pl.program_id(1)
    blocks = SWA_TILE // BLOCK

    def scores(n, g, key_rows, table_rows):
        rows = slice(n * BLOCK, (n + 1) * BLOCK)
        stacked = []
        for j in range(2):
            qp = q_ref[rows, (2 * g + j) * LANES:(2 * g + j + 1) * LANES]
            left = _half_lane_mask(qp.shape)
            zero = jnp.zeros_like(qp)
            stacked += [jnp.where(left, qp, zero), jnp.where(left, zero, qp)]
        kk = kd_ref[key_rows, g * LANES:(g + 1) * LANES]
        bias = jnp.concatenate([tbl_ref[SWA_GROUP * g + j, table_rows, :] for j in range(SWA_GROUP)], axis=1)
        return _dot_nt(kk, jnp.concatenate(stacked, axis=0)) + bias

    def finish(n, g, key_rows, s):
        rows = slice(n * BLOCK, (n + 1) * BLOCK)
        sink = sink_ref[g]
        m = jnp.maximum(jnp.max(s, axis=0, keepdims=True), sink)
        p = jnp.exp2(s - m)
        denom = jnp.sum(p, axis=0, keepdims=True) + jnp.exp2(sink - m)
        p = (p * (1.0 / denom)).astype(BF16)
        vv = vd_ref[key_rows, g * LANES:(g + 1) * LANES]
        o = lax.dot_general(p, vv, (((0,), (0,)), ((), ())), preferred_element_type=F32)
        for j in range(2):
            a = o[(2 * j) * BLOCK:(2 * j + 1) * BLOCK]
            b = o[(2 * j + 1) * BLOCK:(2 * j + 2) * BLOCK]
            left = _half_lane_mask(a.shape)
            o_ref[rows, (2 * g + j) * LANES:(2 * g + j + 1) * LANES] = jnp.where(left, a, b).astype(BF16)

    def run(units):
        s_next = scores(*units[0])
        for k, (n, g, key_rows, _) in enumerate(units):
            s = s_next
            if k + 1 < len(units):
                s_next = scores(*units[k + 1])
            finish(n, g, key_rows, s)

    def with_previous(n):
        start = pl.multiple_of(i * SWA_TILE + (n - 1) * BLOCK, BLOCK)
        return [(n, g, pl.ds(start, 2 * BLOCK), slice(0, 2 * BLOCK)) for g in range(SWA_KV_HEADS)]

    later_blocks = [u for n in range(1, blocks) for u in with_previous(n)]

    @pl.when(i == 0)
    def _():
        run([(0, g, slice(0, BLOCK), slice(BLOCK, 2 * BLOCK)) for g in range(SWA_KV_HEADS)] + later_blocks)

    @pl.when(i > 0)
    def _():
        run(with_previous(0) + later_blocks)


def _swa_attention(qa, kd, vd, tbl, sink):
    b, s, _ = qa.shape
    tq = SWA_TILE
    return pl.pallas_call(
        _swa_kernel,
        grid=(b, s // tq),
        in_specs=[pl.BlockSpec((None, tq, SWA_Q_W), lambda bi, i: (bi, i, 0)),
                  pl.BlockSpec((None, s, 2 * SWA_KV_W), lambda bi, i: (bi, 0, 0)),
                  pl.BlockSpec((None, s, 2 * SWA_KV_W), lambda bi, i: (bi, 0, 0)),
                  pl.BlockSpec(tbl.shape, lambda bi, i: (0, 0, 0)),
                  pl.BlockSpec(sink.shape, lambda bi, i: (0, 0, 0))],
        out_specs=pl.BlockSpec((None, tq, SWA_Q_W), lambda bi, i: (bi, i, 0)),
        out_shape=jax.ShapeDtypeStruct((b, s, SWA_Q_W), BF16),
        compiler_params=pltpu.CompilerParams(dimension_semantics=("parallel", "parallel"),
                                             vmem_limit_bytes=VMEM_LIMIT),
        name="swa_attention",
    )(qa, kd, vd, tbl, sink)


def _diff_query_tile(i, lam, qt_ref, k_ref, vt_ref, tbl_ref, gain_ref, o_ref, m_sc, l_sc, acc_sc, s_sc, smax_sc,
                     lam_init):
    tq = ATTN_TILE
    qt = qt_ref[i]
    top = lax.broadcasted_iota(jnp.int32, qt.shape, 0) < HEAD_DIM
    zero = jnp.zeros_like(qt)
    qts = (jnp.where(top, qt, zero), jnp.where(top, zero, qt))

    def prefetch_scores(j, c):
        kc = k_ref[pl.ds(pl.multiple_of(j * tq, tq), tq), :]
        s = _dot(kc, qts[c])
        s_sc[c] = s
        smax_sc[c] = jnp.max(s, axis=0, keepdims=True)

    def step(j, j_next, bias, first):
        vct = vt_ref[j]
        for c in range(2):
            s = s_sc[c]
            if bias is not None:
                s = s + bias
                m_new = jnp.max(s, axis=0, keepdims=True)
            else:
                m_new = smax_sc[c]
            if not first:
                m_prev = m_sc[c]
                m_new = jnp.maximum(m_prev, m_new)
                alpha = jnp.exp2(m_prev - m_new)
            p = jnp.exp2(s - m_new)
            psum = jnp.sum(p, axis=0, keepdims=True)
            prefetch_scores(j_next, c)
            pv = _dot(vct, p.astype(BF16))
            m_sc[c] = m_new
            if first:
                l_sc[c] = psum
                acc_sc[c] = pv
            else:
                l_sc[c] = alpha * l_sc[c] + psum
                acc_sc[c] = alpha * acc_sc[c] + pv

    def diagonal_step(j_next):
        for c in range(2):
            prefetch_scores(i, c)
        step(i, j_next, tbl_ref[tq:, :], True)

    @pl.when(i == 0)
    def _():
        diagonal_step(0)

    @pl.when(i > 0)
    def _():
        diagonal_step(i - 1)
        step(i - 1, jnp.maximum(i - 2, 0), tbl_ref[:tq, :], False)

    n_far = jnp.maximum(i - 1, 0)
    half = FAR_UNROLL // 2
    n_single = n_far & (half - 1)
    n_half = n_far & half

    def far_steps(j, count):
        for u in range(count):
            step(j - u, jnp.maximum(j - u - 1, 0), None, False)

    def far_single(t, carry):
        far_steps(i - 2 - t, 1)
        return carry

    lax.fori_loop(0, n_single, far_single, 0)

    @pl.when(n_half > 0)
    def _():
        far_steps(i - 2 - n_single, half)

    def far_group(t, carry):
        far_steps(i - 2 - n_single - n_half - FAR_UNROLL * t, FAR_UNROLL)
        return carry

    lax.fori_loop(0, lax.shift_right_logical(n_far, FAR_UNROLL.bit_length() - 1), far_group, 0)

    o = acc_sc[0] / l_sc[0] - lam * (acc_sc[1] / l_sc[1])
    y = o * lax.rsqrt(jnp.mean(o * o, axis=0, keepdims=True) + EPS) * (gain_ref[...] * (1.0 - lam_init))
    o_ref[pl.ds(pl.multiple_of(i * tq, tq), tq), :] = y.astype(BF16).T


def _diff_kernel(lq1_ref, lk1_ref, lq2_ref, lk2_ref, qt_ref, k_ref, vt_ref, tbl_ref, gain_ref, o_ref,
                 m_sc, l_sc, acc_sc, s_sc, smax_sc, *, lam_init):
    lam = (jnp.exp(jnp.sum(lq1_ref[...] * lk1_ref[...], axis=-1, keepdims=True))
           - jnp.exp(jnp.sum(lq2_ref[...] * lk2_ref[...], axis=-1, keepdims=True)) + lam_init)

    def query_tile(i, carry):
        _diff_query_tile(i, lam, qt_ref, k_ref, vt_ref, tbl_ref, gain_ref, o_ref, m_sc, l_sc, acc_sc, s_sc,
                         smax_sc, lam_init)
        return carry

    lax.fori_loop(0, qt_ref.shape[0], query_tile, 0)


def _diff_attention(lam_vecs, qbt, kb, vbt, tbl, gain, lam_init):
    b, s, _ = kb.shape
    tq = ATTN_TILE
    n = s // tq
    vec = pl.BlockSpec((1, HEAD_DIM), lambda bi, h: (0, 0))
    return pl.pallas_call(
        functools.partial(_diff_kernel, lam_init=lam_init),
        grid=(b, DIFF_HEADS),
        in_specs=[vec, vec, vec, vec,
                  pl.BlockSpec((None, n, LANES, tq), lambda bi, h: (bi, 0, h, 0)),
                  pl.BlockSpec((None, s, LANES), lambda bi, h: (bi, 0, h)),
                  pl.BlockSpec((None, n, LANES, tq), lambda bi, h: (bi, 0, h, 0)),
                  pl.BlockSpec((None, 2 * tq, tq), lambda bi, h: (h, 0, 0)),
                  pl.BlockSpec((DIFF_V_DIM, 1), lambda bi, h: (0, 0))],
        out_specs=pl.BlockSpec((None, s, LANES), lambda bi, h: (bi, 0, h)),
        out_shape=jax.ShapeDtypeStruct((b, s, DIFF_V_W), BF16),
        scratch_shapes=[pltpu.VMEM((2, 1, tq), F32), pltpu.VMEM((2, 1, tq), F32),
                        pltpu.VMEM((2, DIFF_V_DIM, tq), F32), pltpu.VMEM((2, tq, tq), F32),
                        pltpu.VMEM((2, 1, tq), F32)],
        compiler_params=pltpu.CompilerParams(dimension_semantics=("parallel", "parallel"),
                                             vmem_limit_bytes=VMEM_LIMIT),
        name="diff_attention",
    )(*lam_vecs, qbt, kb, vbt, tbl, gain)


def _merge_kernel(x_ref, ya_ref, yb_ref, g_ref, wg_ref, wa_ref, wb_ref, wo_ref, o_ref):
    x = x_ref[...]
    h = (x * lax.rsqrt(jnp.mean(x * x, axis=-1, keepdims=True) + EPS) * g_ref[...]).astype(BF16)
    ga = jax.nn.sigmoid(_dot(h, wg_ref[:, :D_MODEL]))
    mixed = ga * _dot(ya_ref[...], wa_ref[...])
    gb = jax.nn.sigmoid(_dot(h, wg_ref[:, D_MODEL:]))
    mixed = mixed + gb * _dot(yb_ref[...], wb_ref[...])
    o_ref[...] = x + _dot(mixed.astype(BF16), wo_ref[...])


def _merge(x2, ya, yb, g_mix, w_gate, wa, wb, wo):
    t = x2.shape[0]
    tm = ROW_TILE
    row = lambda w: pl.BlockSpec((tm, w), lambda i: (i, 0))
    full = lambda a: pl.BlockSpec(a.shape, lambda i: (0, 0), pipeline_mode=pl.Buffered(1))
    return pl.pallas_call(
        _merge_kernel,
        grid=(t // tm,),
        in_specs=[row(D_MODEL), row(SWA_Q_W), row(DIFF_V_W), full(g_mix), full(w_gate), full(wa), full(wb),
                  full(wo)],
        out_specs=row(D_MODEL),
        out_shape=jax.ShapeDtypeStruct((t, D_MODEL), F32),
        compiler_params=pltpu.CompilerParams(dimension_semantics=("parallel",),
                                             vmem_limit_bytes=VMEM_LIMIT),
        name="merge",
    )(x2, ya, yb, g_mix, w_gate, wa, wb, wo)


def _conv_mlp_kernel(x_ref, halo_ref, g_ref, wup_ref, cw_ref, cb_ref, wdn_ref, o_ref, u_sc, acc_sc):
    tm = MLP_TILE
    n_chunks = D_FF // FF_CHUNK
    first = pl.program_id(1) == 0

    def normed(x):
        return (x * lax.rsqrt(jnp.mean(x * x, axis=-1, keepdims=True) + EPS) * g_ref[...]).astype(BF16)

    x = x_ref[...]
    h = normed(x)
    h_halo = normed(halo_ref[...])
    keep = jnp.where(first, 0.0, 1.0)

    def columns(c, part):
        return slice(part * D_FF + c * FF_CHUNK, part * D_FF + (c + 1) * FF_CHUNK)

    def up(c):
        for part in range(2):
            cols = columns(c, part)
            u_sc[c % 2, part, 0:SUBLANES, :] = _dot(h_halo, wup_ref[:, cols]) * keep
            u_sc[c % 2, part, SUBLANES:, :] = _dot(h, wup_ref[:, cols])

    def conv(c, part):
        cols = columns(c, part)
        w = cw_ref[:, cols]
        u = u_sc.at[c % 2, part]
        y = (u[SUBLANES - 2:SUBLANES - 2 + tm, :] * w[0:1]
             + u[SUBLANES - 1:SUBLANES - 1 + tm, :] * w[1:2]
             + u[SUBLANES:SUBLANES + tm, :] * w[2:3])
        return y + cb_ref[:, cols]

    up(0)
    for c in range(n_chunks):
        if c + 1 < n_chunks:
            up(c + 1)
        half_g = 0.5 * conv(c, 0)
        act = ((half_g + half_g * jnp.tanh(half_g)) * conv(c, 1)).astype(BF16)
        contrib = _dot(act, wdn_ref[columns(c, 0), :])
        if c + 1 == n_chunks:
            o_ref[...] = x + acc_sc[...] + contrib
        elif c == 0:
            acc_sc[...] = contrib
        else:
            acc_sc[...] += contrib


def _conv_mlp(x3, g_ffn, w_up, conv_w, conv_b, w_down):
    b, s, _ = x3.shape
    tm = MLP_TILE
    halo_blocks = tm // SUBLANES
    full = lambda a: pl.BlockSpec(a.shape, lambda bi, i: (0,) * a.ndim, pipeline_mode=pl.Buffered(1))
    return pl.pallas_call(
        _conv_mlp_kernel,
        grid=(b, s // tm),
        in_specs=[pl.BlockSpec((None, tm, D_MODEL), lambda bi, i: (bi, i, 0)),
                  pl.BlockSpec((None, SUBLANES, D_MODEL),
                               lambda bi, i: (bi, jnp.maximum(i * halo_blocks - 1, 0), 0)),
                  full(g_ffn), full(w_up), full(conv_w), full(conv_b), full(w_down)],
        out_specs=pl.BlockSpec((None, tm, D_MODEL), lambda bi, i: (bi, i, 0)),
        out_shape=jax.ShapeDtypeStruct((b, s, D_MODEL), F32),
        scratch_shapes=[pltpu.VMEM((2, 2, tm + SUBLANES, FF_CHUNK), F32),
                        pltpu.VMEM((tm, D_MODEL), F32)],
        compiler_params=pltpu.CompilerParams(dimension_semantics=("parallel", "parallel"),
                                             vmem_limit_bytes=VMEM_LIMIT),
        name="conv_mlp",
    )(x3, x3, g_ffn, w_up, conv_w, conv_b, w_down)


def _swa_bucket_idx():
    rel = BLOCK + np.arange(BLOCK)[None, :] - np.arange(2 * BLOCK)[:, None]
    return np.where((rel >= 0) & (rel < WINDOW), _t5_bucket_np(rel), -1).astype(np.int32)


def _diff_bucket_idx():
    tq = ATTN_TILE
    rel = tq + np.arange(tq)[None, :] - np.arange(2 * tq)[:, None]
    full = np.where(rel >= 0, _t5_bucket_np(rel), -1).astype(np.int32)
    nq = tq // BLOCK
    blocks = np.concatenate([full[(nq - 1) * BLOCK:nq * BLOCK, :BLOCK], full[nq * BLOCK:(nq + 1) * BLOCK, :BLOCK]])
    layout = []
    for kb in range(2 * nq):
        row = []
        for qb in range(nq):
            d = kb - qb - nq
            row.append(MASKED_BLOCK if d > 0 else 1 if d == 0 else 0 if d == -1 else ZERO_BLOCK)
        layout.append(row)
    far = N_BUCKETS - 1
    tiled = np.block([[np.full((BLOCK, BLOCK), far) if s == ZERO_BLOCK else np.full((BLOCK, BLOCK), -1)
                       if s == MASKED_BLOCK else blocks[s * BLOCK:(s + 1) * BLOCK] for s in row] for row in layout])
    assert (tiled == full).all()
    return blocks, layout


def _layer(x, l, swa_tbl, diff_tbl, g_mix, w_in, qn_a, kn_a, sinks, qn_b, kn_b, lam_q1, lam_k1,
           lam_q2, lam_k2, subln_b, w_br_a, w_br_b, w_o, g_ffn, w_up, conv_w, conv_b, w_down):
    b, s, _ = x.shape
    t = b * s
    lam_init = 0.8 - 0.6 * math.exp(-0.3 * l)
    q_scale = HEAD_DIM ** -0.5 * LOG2E
    tile = lambda v, reps, scale=1.0: (jnp.tile(v.astype(F32), reps) * scale).reshape(1, -1)

    x2 = x.reshape(t, D_MODEL)
    w_in_bf = w_in[l].astype(BF16)
    g_mix_row = g_mix[l].reshape(1, -1)
    qa, kd, vd, qbt, kb, vbt = _in_proj(
        x2, g_mix_row, w_in_bf[:, :IN_WIDTH - GATE_W],
        tile(qn_a[l], SWA_Q_HEADS, q_scale), tile(kn_a[l], SWA_KV_HEADS),
        tile(qn_b[l], 2 * DIFF_HEADS, q_scale), tile(kn_b[l], 2 * DIFF_HEADS))

    sink = jnp.repeat(sinks[l].astype(F32) * LOG2E, BLOCK).reshape(SWA_KV_HEADS, 1, SWA_GROUP * BLOCK)
    ya = _swa_attention(qa.reshape(b, s, -1), kd.reshape(b, s, -1), vd.reshape(b, s, -1), swa_tbl, sink)

    lam_vecs = [v[l].astype(F32).reshape(1, HEAD_DIM) for v in (lam_q1, lam_k1, lam_q2, lam_k2)]
    chunks = s // ATTN_TILE
    yb = _diff_attention(lam_vecs, qbt.reshape(b, chunks, DIFF_QK_W, ATTN_TILE), kb.reshape(b, s, -1),
                         vbt.reshape(b, chunks, DIFF_V_W, ATTN_TILE), diff_tbl,
                         subln_b[l].astype(F32).reshape(-1, 1), lam_init)

    x2 = _merge(x2, ya.reshape(t, -1), yb.reshape(t, -1), g_mix_row, w_in_bf[:, IN_WIDTH - GATE_W:],
                w_br_a[l].astype(BF16), w_br_b[l].astype(BF16), w_o[l].astype(BF16))
    x3 = _conv_mlp(x2.reshape(b, s, D_MODEL), g_ffn[l].reshape(1, -1), w_up[l].astype(BF16),
                   conv_w[l].astype(F32), conv_b[l].astype(F32).reshape(1, -1), w_down[l].astype(BF16))
    return x3


def kernel(x, rel_bias, g_mix, w_in, qn_a, kn_a, sinks, qn_b, kn_b, lam_q1, lam_k1, lam_q2, lam_k2, subln_b,
           w_br_a, w_br_b, w_o, g_ffn, w_up, conv_w, conv_b, w_down):
    rb = rel_bias.astype(F32)
    swa_tbl = _bias_tables(rb, _swa_bucket_idx(), SWA_Q_HEADS, 0, False)
    diff_idx, diff_layout = _diff_bucket_idx()
    diff_tbl = _bias_tables(rb, diff_idx, DIFF_HEADS, SWA_Q_HEADS, True, diff_layout)
    for l in range(g_mix.shape[0]):
        x = _layer(x, l, swa_tbl, diff_tbl, g_mix, w_in, qn_a, kn_a, sinks, qn_b, kn_b, lam_q1, lam_k1,
                   lam_q2, lam_k2, subln_b, w_br_a, w_br_b, w_o, g_ffn, w_up, conv_w, conv_b, w_down)
    return x
```

```python
import functools
import math

import numpy as np
import jax
import jax.numpy as jnp
from jax import lax
from jax.experimental import pallas as pl
from jax.experimental.pallas import tpu as pltpu

D_MODEL = 1024
HEAD_DIM = 64
SWA_Q_HEADS = 8
SWA_KV_HEADS = 2
SWA_GROUP = SWA_Q_HEADS // SWA_KV_HEADS
WINDOW = 128
BLOCK = 128
DIFF_HEADS = 4
DIFF_V_DIM = 2 * HEAD_DIM
N_BUCKETS = 32
MAX_DISTANCE = 128
D_FF = 2816
CONV_WIDTH = 3
EPS = 1e-6

SWA_Q_W = SWA_Q_HEADS * HEAD_DIM
SWA_KV_W = SWA_KV_HEADS * HEAD_DIM
DIFF_QK_W = DIFF_HEADS * 2 * HEAD_DIM
DIFF_V_W = DIFF_HEADS * DIFF_V_DIM
GATE_W = 2 * D_MODEL
IN_WIDTH = SWA_Q_W + 2 * SWA_KV_W + 2 * DIFF_QK_W + DIFF_V_W + GATE_W

LANES = 128
SUBLANES = 8
LOG2E = math.log2(math.e)
NEG_INF = float("-inf")

ROW_TILE = 1024
MLP_TILE = 512
SWA_TILE = 1024
ATTN_TILE = 512
FF_CHUNK = 256
FAR_UNROLL = 8
VMEM_LIMIT = 56 * 1024 * 1024

F32 = jnp.float32
BF16 = jnp.bfloat16

DIAGONAL, PREVIOUS, FAR = "diagonal", "previous", "far"


def _t5_bucket_np(rel):
    n = np.maximum(rel, 0)
    max_exact = N_BUCKETS // 2
    nf = np.maximum(n, 1).astype(np.float64)
    large = max_exact + (np.log(nf / max_exact) / math.log(MAX_DISTANCE / max_exact)
                         * (N_BUCKETS - max_exact)).astype(np.int32)
    large = np.minimum(large, N_BUCKETS - 1)
    return np.where(n < max_exact, n, large).astype(np.int32)


def _dot(a, b):
    return jnp.dot(a, b, preferred_element_type=F32)


def _dot_nt(a, b):
    return lax.dot_general(a, b, (((1,), (1,)), ((), ())), preferred_element_type=F32)


def _half_lane_mask(shape):
    return lax.broadcasted_iota(jnp.int32, shape, len(shape) - 1) < HEAD_DIM


def _bias_table_kernel(rb_ref, idx_ref, out_ref, *, col0, shift_far):
    h = pl.program_id(0) + col0
    idx = idx_ref[...]
    shift = rb_ref[N_BUCKETS - 1, h] if shift_far else 0.0
    acc = jnp.full(idx.shape, NEG_INF, F32)
    for b in range(N_BUCKETS):
        acc = jnp.where(idx == b, (rb_ref[b, h] - shift) * LOG2E, acc)
    out_ref[...] = acc


def _bias_tables(rel_bias, idx, n_heads, col0, shift_far):
    r, c = idx.shape
    return pl.pallas_call(
        functools.partial(_bias_table_kernel, col0=col0, shift_far=shift_far),
        grid=(n_heads,),
        in_specs=[pl.BlockSpec(memory_space=pltpu.SMEM),
                  pl.BlockSpec((r, c), lambda h: (0, 0))],
        out_specs=pl.BlockSpec((None, r, c), lambda h: (h, 0, 0)),
        out_shape=jax.ShapeDtypeStruct((n_heads, r, c), F32),
        name="bias_tables",
    )(rel_bias, jnp.asarray(idx))


def _pair_rms_norm(z, gain):
    outs = []
    for j in range(z.shape[1] // LANES):
        s = z[:, j * LANES:(j + 1) * LANES]
        left = _half_lane_mask(s.shape)
        sq = s * s
        tot = jnp.sum(sq, axis=-1, keepdims=True)
        lo = jnp.sum(jnp.where(left, sq, 0.0), axis=-1, keepdims=True)
        ms = jnp.where(left, lo, tot - lo) * (1.0 / HEAD_DIM)
        outs.append(s * lax.rsqrt(ms + EPS))
    return jnp.concatenate(outs, axis=-1) * gain


def _dup_halves(z):
    left = _half_lane_mask(z.shape)
    rolled = pltpu.roll(z, HEAD_DIM, axis=1)
    return jnp.concatenate([jnp.where(left, z, rolled), jnp.where(left, rolled, z)], axis=-1)


def _in_proj_kernel(x_ref, g_ref, w_ref, gqa_ref, gka_ref, gqb_ref, gkb_ref,
                    qa_ref, kd_ref, vd_ref, qbt_ref, kb_ref, vbt_ref):
    x = x_ref[...]
    h = (x * lax.rsqrt(jnp.mean(x * x, axis=-1, keepdims=True) + EPS) * g_ref[...]).astype(BF16)

    widths = (SWA_Q_W, SWA_KV_W, SWA_KV_W, DIFF_QK_W, DIFF_QK_W, DIFF_V_W)
    starts = dict(zip(("qa", "ka", "va", "qb", "kb", "vb"), np.cumsum((0,) + widths[:-1])))

    def proj(name, width):
        return _dot(h, w_ref[:, starts[name]:starts[name] + width])

    z_kb = proj("kb", DIFF_QK_W)
    z_qb = proj("qb", DIFF_QK_W)
    kb_ref[...] = _pair_rms_norm(z_kb, gkb_ref[...]).astype(BF16)
    z_qa = proj("qa", SWA_Q_W)
    qbt_ref[...] = _pair_rms_norm(z_qb, gqb_ref[...]).astype(BF16).T
    z_ka = proj("ka", SWA_KV_W)
    qa_ref[...] = _pair_rms_norm(z_qa, gqa_ref[...]).astype(BF16)
    z_vb = proj("vb", DIFF_V_W)
    kd_ref[...] = _dup_halves(_pair_rms_norm(z_ka, gka_ref[...])).astype(BF16)
    z_va = proj("va", SWA_KV_W)
    vbt_ref[...] = z_vb.astype(BF16).T
    vd_ref[...] = _dup_halves(z_va).astype(BF16)


def _in_proj(x2, g_mix, w_in, gqa, gka, gqb, gkb):
    t = x2.shape[0]
    tm = ATTN_TILE
    row = lambda w: pl.BlockSpec((tm, w), lambda i: (i, 0))
    full = lambda a: pl.BlockSpec(a.shape, lambda i: (0, 0), pipeline_mode=pl.Buffered(1))
    tposed = lambda w: pl.BlockSpec((None, w, tm), lambda i: (i, 0, 0))
    bf = lambda w: jax.ShapeDtypeStruct((t, w), BF16)
    bft = lambda w: jax.ShapeDtypeStruct((t // tm, w, tm), BF16)
    return pl.pallas_call(
        _in_proj_kernel,
        grid=(t // tm,),
        in_specs=[row(D_MODEL), full(g_mix), full(w_in), full(gqa), full(gka), full(gqb), full(gkb)],
        out_specs=[row(SWA_Q_W), row(2 * SWA_KV_W), row(2 * SWA_KV_W), tposed(DIFF_QK_W), row(DIFF_QK_W),
                   tposed(DIFF_V_W)],
        out_shape=[bf(SWA_Q_W), bf(2 * SWA_KV_W), bf(2 * SWA_KV_W), bft(DIFF_QK_W), bf(DIFF_QK_W),
                   bft(DIFF_V_W)],
        compiler_params=pltpu.CompilerParams(dimension_semantics=("parallel",),
                                             vmem_limit_bytes=VMEM_LIMIT),
        name="in_proj",
    )(x2, g_mix, w_in, gqa, gka, gqb, gkb)


def _swa_kernel(q_ref, kd_ref, vd_ref, tbl_ref, sink_ref, o_ref):
    i = pl.program_id(1)
    blocks = SWA_TILE // BLOCK

    def scores(n, g, key_rows, table_rows):
        rows = slice(n * BLOCK, (n + 1) * BLOCK)
        stacked = []
        for j in range(2):
            qp = q_ref[rows, (2 * g + j) * LANES:(2 * g + j + 1) * LANES]
            left = _half_lane_mask(qp.shape)
            zero = jnp.zeros_like(qp)
            stacked += [jnp.where(left, qp, zero), jnp.where(left, zero, qp)]
        kk = kd_ref[key_rows, g * LANES:(g + 1) * LANES]
        bias = jnp.concatenate([tbl_ref[SWA_GROUP * g + j, table_rows, :] for j in range(SWA_GROUP)], axis=1)
        return _dot_nt(kk, jnp.concatenate(stacked, axis=0)) + bias

    def finish(n, g, key_rows, s):
        rows = slice(n * BLOCK, (n + 1) * BLOCK)
        sink = sink_ref[g]
        m = jnp.maximum(jnp.max(s, axis=0, keepdims=True), sink)
        p = jnp.exp2(s - m)
        denom = jnp.sum(p, axis=0, keepdims=True) + jnp.exp2(sink - m)
        p = (p * (1.0 / denom)).astype(BF16)
        vv = vd_ref[key_rows, g * LANES:(g + 1) * LANES]
        o = lax.dot_general(p, vv, (((0,), (0,)), ((), ())), preferred_element_type=F32)
        for j in range(2):
            a = o[(2 * j) * BLOCK:(2 * j + 1) * BLOCK]
            b = o[(2 * j + 1) * BLOCK:(2 * j + 2) * BLOCK]
            left = _half_lane_mask(a.shape)
            o_ref[rows, (2 * g + j) * LANES:(2 * g + j + 1) * LANES] = jnp.where(left, a, b).astype(BF16)

    def run(units):
        s_next = scores(*units[0])
        for k, (n, g, key_rows, _) in enumerate(units):
            s = s_next
            if k + 1 < len(units):
                s_next = scores(*units[k + 1])
            finish(n, g, key_rows, s)

    def with_previous(n):
        start = pl.multiple_of(i * SWA_TILE + (n - 1) * BLOCK, BLOCK)
        return [(n, g, pl.ds(start, 2 * BLOCK), slice(0, 2 * BLOCK)) for g in range(SWA_KV_HEADS)]

    later_blocks = [u for n in range(1, blocks) for u in with_previous(n)]

    @pl.when(i == 0)
    def _():
        run([(0, g, slice(0, BLOCK), slice(BLOCK, 2 * BLOCK)) for g in range(SWA_KV_HEADS)] + later_blocks)

    @pl.when(i > 0)
    def _():
        run(with_previous(0) + later_blocks)


def _swa_attention(qa, kd, vd, tbl, sink):
    b, s, _ = qa.shape
    tq = SWA_TILE
    return pl.pallas_call(
        _swa_kernel,
        grid=(b, s // tq),
        in_specs=[pl.BlockSpec((None, tq, SWA_Q_W), lambda bi, i: (bi, i, 0)),
                  pl.BlockSpec((None, s, 2 * SWA_KV_W), lambda bi, i: (bi, 0, 0)),
                  pl.BlockSpec((None, s, 2 * SWA_KV_W), lambda bi, i: (bi, 0, 0)),
                  pl.BlockSpec(tbl.shape, lambda bi, i: (0, 0, 0)),
                  pl.BlockSpec(sink.shape, lambda bi, i: (0, 0, 0))],
        out_specs=pl.BlockSpec((None, tq, SWA_Q_W), lambda bi, i: (bi, i, 0)),
        out_shape=jax.ShapeDtypeStruct((b, s, SWA_Q_W), BF16),
        compiler_params=pltpu.CompilerParams(dimension_semantics=("parallel", "parallel"),
                                             vmem_limit_bytes=VMEM_LIMIT),
        name="swa_attention",
    )(qa, kd, vd, tbl, sink)


def _diff_query_tile(i, lam, qt_ref, k_ref, vt_ref, tbl_ref, gain_ref, o_ref, m_sc, l_sc, acc_sc, s_sc, smax_sc,
                     lam_init):
    tq = ATTN_TILE
    qt = qt_ref[i]
    top = lax.broadcasted_iota(jnp.int32, qt.shape, 0) < HEAD_DIM
    zero = jnp.zeros_like(qt)
    qts = (jnp.where(top, qt, zero), jnp.where(top, zero, qt))

    def prefetch_scores(j, c):
        kc = k_ref[pl.ds(pl.multiple_of(j * tq, tq), tq), :]
        s = _dot(kc, qts[c])
        s_sc[c] = s
        smax_sc[c] = jnp.max(s, axis=0, keepdims=True)

    def biased_tile(c, kind):
        s = s_sc[c]
        nq = tq // BLOCK
        blk = lambda kb, qb: s[kb * BLOCK:(kb + 1) * BLOCK, qb * BLOCK:(qb + 1) * BLOCK]
        band, diag = tbl_ref[:BLOCK, :], tbl_ref[BLOCK:, :]
        if kind == DIAGONAL:
            return [[blk(kb, qb) + diag if kb == qb else blk(kb, qb) + band if kb == qb - 1 else blk(kb, qb)
                     for kb in range(qb + 1)] for qb in range(nq)]
        return [[blk(kb, qb) + band if (kb, qb) == (nq - 1, 0) else blk(kb, qb) for kb in range(nq)]
                for qb in range(nq)]

    def softmax(c, kind, first):
        if kind == FAR:
            s = s_sc[c]
            m_new = smax_sc[c]
        else:
            cols = biased_tile(c, kind)
            m_new = jnp.concatenate(
                [functools.reduce(jnp.maximum, [jnp.max(b, axis=0, keepdims=True) for b in col]) for col in cols],
                axis=1)
        alpha = None
        if not first:
            m_prev = m_sc[c]
            m_new = jnp.maximum(m_prev, m_new)
            alpha = jnp.exp2(m_prev - m_new)
        if kind == FAR:
            p = jnp.exp2(s - m_new)
            return m_new, alpha, p, jnp.sum(p, axis=0, keepdims=True)
        p_cols, sums = [], []
        for qb, col in enumerate(cols):
            m_col = m_new[:, qb * BLOCK:(qb + 1) * BLOCK]
            ps = [jnp.exp2(b - m_col) for b in col]
            sums.append(sum(jnp.sum(p, axis=0, keepdims=True) for p in ps))
            masked = [jnp.zeros((BLOCK, BLOCK), F32)] * (tq // BLOCK - len(col))
            p_cols.append(jnp.concatenate(ps + masked, axis=0))
        return m_new, alpha, jnp.concatenate(p_cols, axis=1), jnp.concatenate(sums, axis=1)

    def step(j, j_next, kind, first):
        vct = vt_ref[j]
        for c in range(2):
            m_new, alpha, p, psum = softmax(c, kind, first)
            prefetch_scores(j_next, c)
            pv = _dot(vct, p.astype(BF16))
            m_sc[c] = m_new
            if first:
                l_sc[c] = psum
                acc_sc[c] = pv
            else:
                l_sc[c] = alpha * l_sc[c] + psum
                acc_sc[c] = alpha * acc_sc[c] + pv

    def diagonal_step(j_next):
        for c in range(2):
            prefetch_scores(i, c)
        step(i, j_next, DIAGONAL, True)

    @pl.when(i == 0)
    def _():
        diagonal_step(0)

    @pl.when(i > 0)
    def _():
        diagonal_step(i - 1)
        step(i - 1, jnp.maximum(i - 2, 0), PREVIOUS, False)

    n_far = jnp.maximum(i - 1, 0)
    half = FAR_UNROLL // 2
    n_single = n_far & (half - 1)
    n_half = n_far & half

    def far_steps(j, count):
        for u in range(count):
            step(j - u, jnp.maximum(j - u - 1, 0), FAR, False)

    def far_single(t, carry):
        far_steps(i - 2 - t, 1)
        return carry

    lax.fori_loop(0, n_single, far_single, 0)

    @pl.when(n_half > 0)
    def _():
        far_steps(i - 2 - n_single, half)

    def far_group(t, carry):
        far_steps(i - 2 - n_single - n_half - FAR_UNROLL * t, FAR_UNROLL)
        return carry

    lax.fori_loop(0, lax.shift_right_logical(n_far, FAR_UNROLL.bit_length() - 1), far_group, 0)

    o = acc_sc[0] / l_sc[0] - lam * (acc_sc[1] / l_sc[1])
    y = o * lax.rsqrt(jnp.mean(o * o, axis=0, keepdims=True) + EPS) * (gain_ref[...] * (1.0 - lam_init))
    o_ref[pl.ds(pl.multiple_of(i * tq, tq), tq), :] = y.astype(BF16).T


def _diff_kernel(lq1_ref, lk1_ref, lq2_ref, lk2_ref, qt_ref, k_ref, vt_ref, tbl_ref, gain_ref, o_ref,
                 m_sc, l_sc, acc_sc, s_sc, smax_sc, *, lam_init):
    lam = (jnp.exp(jnp.sum(lq1_ref[...] * lk1_ref[...], axis=-1, keepdims=True))
           - jnp.exp(jnp.sum(lq2_ref[...] * lk2_ref[...], axis=-1, keepdims=True)) + lam_init)

    def query_tile(i, carry):
        _diff_query_tile(i, lam, qt_ref, k_ref, vt_ref, tbl_ref, gain_ref, o_ref, m_sc, l_sc, acc_sc, s_sc,
                         smax_sc, lam_init)
        return carry

    lax.fori_loop(0, qt_ref.shape[0], query_tile, 0)


def _diff_attention(lam_vecs, qbt, kb, vbt, tbl, gain, lam_init):
    b, s, _ = kb.shape
    tq = ATTN_TILE
    n = s // tq
    vec = pl.BlockSpec((1, HEAD_DIM), lambda bi, h: (0, 0))
    return pl.pallas_call(
        functools.partial(_diff_kernel, lam_init=lam_init),
        grid=(b, DIFF_HEADS),
        in_specs=[vec, vec, vec, vec,
                  pl.BlockSpec((None, n, LANES, tq), lambda bi, h: (bi, 0, h, 0)),
                  pl.BlockSpec((None, s, LANES), lambda bi, h: (bi, 0, h)),
                  pl.BlockSpec((None, n, LANES, tq), lambda bi, h: (bi, 0, h, 0)),
                  pl.BlockSpec((None, 2 * BLOCK, BLOCK), lambda bi, h: (h, 0, 0)),
                  pl.BlockSpec((DIFF_V_DIM, 1), lambda bi, h: (0, 0))],
        out_specs=pl.BlockSpec((None, s, LANES), lambda bi, h: (bi, 0, h)),
        out_shape=jax.ShapeDtypeStruct((b, s, DIFF_V_W), BF16),
        scratch_shapes=[pltpu.VMEM((2, 1, tq), F32), pltpu.VMEM((2, 1, tq), F32),
                        pltpu.VMEM((2, DIFF_V_DIM, tq), F32), pltpu.VMEM((2, tq, tq), F32),
                        pltpu.VMEM((2, 1, tq), F32)],
        compiler_params=pltpu.CompilerParams(dimension_semantics=("parallel", "parallel"),
                                             vmem_limit_bytes=VMEM_LIMIT),
        name="diff_attention",
    )(*lam_vecs, qbt, kb, vbt, tbl, gain)


def _merge_kernel(x_ref, ya_ref, yb_ref, g_ref, wg_ref, wa_ref, wb_ref, wo_ref, o_ref):
    x = x_ref[...]
    h = (x * lax.rsqrt(jnp.mean(x * x, axis=-1, keepdims=True) + EPS) * g_ref[...]).astype(BF16)
    ga = jax.nn.sigmoid(_dot(h, wg_ref[:, :D_MODEL]))
    mixed = ga * _dot(ya_ref[...], wa_ref[...])
    gb = jax.nn.sigmoid(_dot(h, wg_ref[:, D_MODEL:]))
    mixed = mixed + gb * _dot(yb_ref[...], wb_ref[...])
    o_ref[...] = x + _dot(mixed.astype(BF16), wo_ref[...])


def _merge(x2, ya, yb, g_mix, w_gate, wa, wb, wo):
    t = x2.shape[0]
    tm = ROW_TILE
    row = lambda w: pl.BlockSpec((tm, w), lambda i: (i, 0))
    full = lambda a: pl.BlockSpec(a.shape, lambda i: (0, 0), pipeline_mode=pl.Buffered(1))
    return pl.pallas_call(
        _merge_kernel,
        grid=(t // tm,),
        in_specs=[row(D_MODEL), row(SWA_Q_W), row(DIFF_V_W), full(g_mix), full(w_gate), full(wa), full(wb),
                  full(wo)],
        out_specs=row(D_MODEL),
        out_shape=jax.ShapeDtypeStruct((t, D_MODEL), F32),
        compiler_params=pltpu.CompilerParams(dimension_semantics=("parallel",),
                                             vmem_limit_bytes=VMEM_LIMIT),
        name="merge",
    )(x2, ya, yb, g_mix, w_gate, wa, wb, wo)


def _conv_mlp_kernel(x_ref, halo_ref, g_ref, wup_ref, cw_ref, cb_ref, wdn_ref, o_ref, u_sc, acc_sc):
    tm = MLP_TILE
    n_chunks = D_FF // FF_CHUNK
    first = pl.program_id(1) == 0

    def normed(x):
        return (x * lax.rsqrt(jnp.mean(x * x, axis=-1, keepdims=True) + EPS) * g_ref[...]).astype(BF16)

    x = x_ref[...]
    h = normed(x)
    h_halo = normed(halo_ref[...])
    keep = jnp.where(first, 0.0, 1.0)

    def columns(c, part):
        return slice(part * D_FF + c * FF_CHUNK, part * D_FF + (c + 1) * FF_CHUNK)

    def up(c):
        for part in range(2):
            cols = columns(c, part)
            u_sc[c % 2, part, 0:SUBLANES, :] = _dot(h_halo, wup_ref[:, cols]) * keep
            u_sc[c % 2, part, SUBLANES:, :] = _dot(h, wup_ref[:, cols])

    def conv(c, part):
        cols = columns(c, part)
        w = cw_ref[:, cols]
        u = u_sc.at[c % 2, part]
        y = (u[SUBLANES - 2:SUBLANES - 2 + tm, :] * w[0:1]
             + u[SUBLANES - 1:SUBLANES - 1 + tm, :] * w[1:2]
             + u[SUBLANES:SUBLANES + tm, :] * w[2:3])
        return y + cb_ref[:, cols]

    up(0)
    for c in range(n_chunks):
        if c + 1 < n_chunks:
            up(c + 1)
        half_g = 0.5 * conv(c, 0)
        act = ((half_g + half_g * jnp.tanh(half_g)) * conv(c, 1)).astype(BF16)
        contrib = _dot(act, wdn_ref[columns(c, 0), :])
        if c + 1 == n_chunks:
            o_ref[...] = x + acc_sc[...] + contrib
        elif c == 0:
            acc_sc[...] = contrib
        else:
            acc_sc[...] += contrib


def _conv_mlp(x3, g_ffn, w_up, conv_w, conv_b, w_down):
    b, s, _ = x3.shape
    tm = MLP_TILE
    halo_blocks = tm // SUBLANES
    full = lambda a: pl.BlockSpec(a.shape, lambda bi, i: (0,) * a.ndim, pipeline_mode=pl.Buffered(1))
    return pl.pallas_call(
        _conv_mlp_kernel,
        grid=(b, s // tm),
        in_specs=[pl.BlockSpec((None, tm, D_MODEL), lambda bi, i: (bi, i, 0)),
                  pl.BlockSpec((None, SUBLANES, D_MODEL),
                               lambda bi, i: (bi, jnp.maximum(i * halo_blocks - 1, 0), 0)),
                  full(g_ffn), full(w_up), full(conv_w), full(conv_b), full(w_down)],
        out_specs=pl.BlockSpec((None, tm, D_MODEL), lambda bi, i: (bi, i, 0)),
        out_shape=jax.ShapeDtypeStruct((b, s, D_MODEL), F32),
        scratch_shapes=[pltpu.VMEM((2, 2, tm + SUBLANES, FF_CHUNK), F32),
                        pltpu.VMEM((tm, D_MODEL), F32)],
        compiler_params=pltpu.CompilerParams(dimension_semantics=("parallel", "parallel"),
                                             vmem_limit_bytes=VMEM_LIMIT),
        name="conv_mlp",
    )(x3, x3, g_ffn, w_up, conv_w, conv_b, w_down)


def _swa_bucket_idx():
    rel = BLOCK + np.arange(BLOCK)[None, :] - np.arange(2 * BLOCK)[:, None]
    return np.where((rel >= 0) & (rel < WINDOW), _t5_bucket_np(rel), -1).astype(np.int32)


def _diff_bucket_idx():
    rel = np.arange(BLOCK)[None, :] - np.arange(BLOCK)[:, None]
    below = _t5_bucket_np(rel + BLOCK)
    diagonal = np.where(rel >= 0, _t5_bucket_np(rel), -1)
    return np.concatenate([below, diagonal]).astype(np.int32)


def _layer(x, l, swa_tbl, diff_tbl, g_mix, w_in, qn_a, kn_a, sinks, qn_b, kn_b, lam_q1, lam_k1,
           lam_q2, lam_k2, subln_b, w_br_a, w_br_b, w_o, g_ffn, w_up, conv_w, conv_b, w_down):
    b, s, _ = x.shape
    t = b * s
    lam_init = 0.8 - 0.6 * math.exp(-0.3 * l)
    q_scale = HEAD_DIM ** -0.5 * LOG2E
    tile = lambda v, reps, scale=1.0: (jnp.tile(v.astype(F32), reps) * scale).reshape(1, -1)

    x2 = x.reshape(t, D_MODEL)
    w_in_bf = w_in[l].astype(BF16)
    g_mix_row = g_mix[l].reshape(1, -1)
    qa, kd, vd, qbt, kb, vbt = _in_proj(
        x2, g_mix_row, w_in_bf[:, :IN_WIDTH - GATE_W],
        tile(qn_a[l], SWA_Q_HEADS, q_scale), tile(kn_a[l], SWA_KV_HEADS),
        tile(qn_b[l], 2 * DIFF_HEADS, q_scale), tile(kn_b[l], 2 * DIFF_HEADS))

    sink = jnp.repeat(sinks[l].astype(F32) * LOG2E, BLOCK).reshape(SWA_KV_HEADS, 1, SWA_GROUP * BLOCK)
    ya = _swa_attention(qa.reshape(b, s, -1), kd.reshape(b, s, -1), vd.reshape(b, s, -1), swa_tbl, sink)

    lam_vecs = [v[l].astype(F32).reshape(1, HEAD_DIM) for v in (lam_q1, lam_k1, lam_q2, lam_k2)]
    chunks = s // ATTN_TILE
    yb = _diff_attention(lam_vecs, qbt.reshape(b, chunks, DIFF_QK_W, ATTN_TILE), kb.reshape(b, s, -1),
                         vbt.reshape(b, chunks, DIFF_V_W, ATTN_TILE), diff_tbl,
                         subln_b[l].astype(F32).reshape(-1, 1), lam_init)

    x2 = _merge(x2, ya.reshape(t, -1), yb.reshape(t, -1), g_mix_row, w_in_bf[:, IN_WIDTH - GATE_W:],
                w_br_a[l].astype(BF16), w_br_b[l].astype(BF16), w_o[l].astype(BF16))
    x3 = _conv_mlp(x2.reshape(b, s, D_MODEL), g_ffn[l].reshape(1, -1), w_up[l].astype(BF16),
                   conv_w[l].astype(F32), conv_b[l].astype(F32).reshape(1, -1), w_down[l].astype(BF16))
    return x3


def kernel(x, rel_bias, g_mix, w_in, qn_a, kn_a, sinks, qn_b, kn_b, lam_q1, lam_k1, lam_q2, lam_k2, subln_b,
           w_br_a, w_br_b, w_o, g_ffn, w_up, conv_w, conv_b, w_down):
    rb = rel_bias.astype(F32)
    swa_tbl = _bias_tables(rb, _swa_bucket_idx(), SWA_Q_HEADS, 0, False)
    diff_tbl = _bias_tables(rb, _diff_bucket_idx(), DIFF_HEADS, SWA_Q_HEADS, True)
    for l in range(g_mix.shape[0]):
        x = _layer(x, l, swa_tbl, diff_tbl, g_mix, w_in, qn_a, kn_a, sinks, qn_b, kn_b, lam_q1, lam_k1,
                   lam_q2, lam_k2, subln_b, w_br_a, w_br_b, w_o, g_ffn, w_up, conv_w, conv_b, w_down)
    return x
```

```python
import functools
import math

import numpy as np
import jax
import jax.numpy as jnp
from jax import lax
from jax.experimental import pallas as pl
from jax.experimental.pallas import tpu as pltpu

D_MODEL = 1024
HEAD_DIM = 64
SWA_Q_HEADS = 8
SWA_KV_HEADS = 2
SWA_GROUP = SWA_Q_HEADS // SWA_KV_HEADS
WINDOW = 128
BLOCK = 128
DIFF_HEADS = 4
DIFF_V_DIM = 2 * HEAD_DIM
N_BUCKETS = 32
MAX_DISTANCE = 128
D_FF = 2816
CONV_WIDTH = 3
EPS = 1e-6

SWA_Q_W = SWA_Q_HEADS * HEAD_DIM
SWA_KV_W = SWA_KV_HEADS * HEAD_DIM
DIFF_QK_W = DIFF_HEADS * 2 * HEAD_DIM
DIFF_V_W = DIFF_HEADS * DIFF_V_DIM
GATE_W = 2 * D_MODEL
IN_WIDTH = SWA_Q_W + 2 * SWA_KV_W + 2 * DIFF_QK_W + DIFF_V_W + GATE_W

LANES = 128
SUBLANES = 8
LOG2E = math.log2(math.e)
NEG_INF = float("-inf")

ROW_TILE = 1024
MLP_TILE = 512
SWA_TILE = 1024
ATTN_TILE = 512
FF_CHUNK = 256
FAR_UNROLL = 8
VMEM_LIMIT = 56 * 1024 * 1024

F32 = jnp.float32
BF16 = jnp.bfloat16

DIAGONAL, PREVIOUS, FAR = "diagonal", "previous", "far"


def _t5_bucket_np(rel):
    n = np.maximum(rel, 0)
    max_exact = N_BUCKETS // 2
    nf = np.maximum(n, 1).astype(np.float64)
    large = max_exact + (np.log(nf / max_exact) / math.log(MAX_DISTANCE / max_exact)
                         * (N_BUCKETS - max_exact)).astype(np.int32)
    large = np.minimum(large, N_BUCKETS - 1)
    return np.where(n < max_exact, n, large).astype(np.int32)


def _dot(a, b):
    return jnp.dot(a, b, preferred_element_type=F32)


def _dot_nt(a, b):
    return lax.dot_general(a, b, (((1,), (1,)), ((), ())), preferred_element_type=F32)


def _half_lane_mask(shape):
    return lax.broadcasted_iota(jnp.int32, shape, len(shape) - 1) < HEAD_DIM


def _bias_table_kernel(rb_ref, idx_ref, out_ref, *, col0, shift_far):
    h = pl.program_id(0) + col0
    idx = idx_ref[...]
    shift = rb_ref[N_BUCKETS - 1, h] if shift_far else 0.0
    acc = jnp.full(idx.shape, NEG_INF, F32)
    for b in range(N_BUCKETS):
        acc = jnp.where(idx == b, (rb_ref[b, h] - shift) * LOG2E, acc)
    out_ref[...] = acc


def _bias_tables(rel_bias, idx, n_heads, col0, shift_far):
    r, c = idx.shape
    return pl.pallas_call(
        functools.partial(_bias_table_kernel, col0=col0, shift_far=shift_far),
        grid=(n_heads,),
        in_specs=[pl.BlockSpec(memory_space=pltpu.SMEM),
                  pl.BlockSpec((r, c), lambda h: (0, 0))],
        out_specs=pl.BlockSpec((None, r, c), lambda h: (h, 0, 0)),
        out_shape=jax.ShapeDtypeStruct((n_heads, r, c), F32),
        name="bias_tables",
    )(rel_bias, jnp.asarray(idx))


def _pair_rms_norm(z, gain):
    outs = []
    for j in range(z.shape[1] // LANES):
        s = z[:, j * LANES:(j + 1) * LANES]
        left = _half_lane_mask(s.shape)
        sq = s * s
        tot = jnp.sum(sq, axis=-1, keepdims=True)
        lo = jnp.sum(jnp.where(left, sq, 0.0), axis=-1, keepdims=True)
        ms = jnp.where(left, lo, tot - lo) * (1.0 / HEAD_DIM)
        outs.append(s * lax.rsqrt(ms + EPS))
    return jnp.concatenate(outs, axis=-1) * gain


def _dup_halves(z):
    left = _half_lane_mask(z.shape)
    rolled = pltpu.roll(z, HEAD_DIM, axis=1)
    return jnp.concatenate([jnp.where(left, z, rolled), jnp.where(left, rolled, z)], axis=-1)


def _in_proj_kernel(x_ref, g_ref, w_ref, gqa_ref, gka_ref, gqb_ref, gkb_ref,
                    qa_ref, kd_ref, vd_ref, qbt_ref, kb_ref, vbt_ref):
    x = x_ref[...]
    h = (x * lax.rsqrt(jnp.mean(x * x, axis=-1, keepdims=True) + EPS) * g_ref[...]).astype(BF16)

    widths = (SWA_Q_W, SWA_KV_W, SWA_KV_W, DIFF_QK_W, DIFF_QK_W, DIFF_V_W)
    starts = dict(zip(("qa", "ka", "va", "qb", "kb", "vb"), np.cumsum((0,) + widths[:-1])))

    def proj(name, width):
        return _dot(h, w_ref[:, starts[name]:starts[name] + width])

    z_kb = proj("kb", DIFF_QK_W)
    z_qb = proj("qb", DIFF_QK_W)
    kb_ref[...] = _pair_rms_norm(z_kb, gkb_ref[...]).astype(BF16)
    z_qa = proj("qa", SWA_Q_W)
    qbt_ref[...] = _pair_rms_norm(z_qb, gqb_ref[...]).astype(BF16).T
    z_ka = proj("ka", SWA_KV_W)
    qa_ref[...] = _pair_rms_norm(z_qa, gqa_ref[...]).astype(BF16)
    z_vb = proj("vb", DIFF_V_W)
    kd_ref[...] = _dup_halves(_pair_rms_norm(z_ka, gka_ref[...])).astype(BF16)
    z_va = proj("va", SWA_KV_W)
    vbt_ref[...] = z_vb.astype(BF16).T
    vd_ref[...] = _dup_halves(z_va).astype(BF16)


def _in_proj(x2, g_mix, w_in, gqa, gka, gqb, gkb):
    t = x2.shape[0]
    tm = ATTN_TILE
    row = lambda w: pl.BlockSpec((tm, w), lambda i: (i, 0))
    full = lambda a: pl.BlockSpec(a.shape, lambda i: (0, 0), pipeline_mode=pl.Buffered(1))
    tposed = lambda w: pl.BlockSpec((None, w, tm), lambda i: (i, 0, 0))
    bf = lambda w: jax.ShapeDtypeStruct((t, w), BF16)
    bft = lambda w: jax.ShapeDtypeStruct((t // tm, w, tm), BF16)
    return pl.pallas_call(
        _in_proj_kernel,
        grid=(t // tm,),
        in_specs=[row(D_MODEL), full(g_mix), full(w_in), full(gqa), full(gka), full(gqb), full(gkb)],
        out_specs=[row(SWA_Q_W), row(2 * SWA_KV_W), row(2 * SWA_KV_W), tposed(DIFF_QK_W), row(DIFF_QK_W),
                   tposed(DIFF_V_W)],
        out_shape=[bf(SWA_Q_W), bf(2 * SWA_KV_W), bf(2 * SWA_KV_W), bft(DIFF_QK_W), bf(DIFF_QK_W),
                   bft(DIFF_V_W)],
        compiler_params=pltpu.CompilerParams(dimension_semantics=("parallel",),
                                             vmem_limit_bytes=VMEM_LIMIT),
        name="in_proj",
    )(x2, g_mix, w_in, gqa, gka, gqb, gkb)


def _swa_kernel(q_ref, kd_ref, vd_ref, tbl_ref, sink_ref, o_ref):
    i = pl.program_id(1)
    blocks = SWA_TILE // BLOCK

    def scores(n, g, key_rows, table_rows):
        rows = slice(n * BLOCK, (n + 1) * BLOCK)
        stacked = []
        for j in range(2):
            qp = q_ref[rows, (2 * g + j) * LANES:(2 * g + j + 1) * LANES]
            left = _half_lane_mask(qp.shape)
            zero = jnp.zeros_like(qp)
            stacked += [jnp.where(left, qp, zero), jnp.where(left, zero, qp)]
        kk = kd_ref[key_rows, g * LANES:(g + 1) * LANES]
        bias = jnp.concatenate([tbl_ref[SWA_GROUP * g + j, table_rows, :] for j in range(SWA_GROUP)], axis=1)
        return _dot_nt(kk, jnp.concatenate(stacked, axis=0)) + bias

    def finish(n, g, key_rows, s):
        rows = slice(n * BLOCK, (n + 1) * BLOCK)
        sink = sink_ref[g]
        m = jnp.maximum(jnp.max(s, axis=0, keepdims=True), sink)
        p = jnp.exp2(s - m)
        denom = jnp.sum(p, axis=0, keepdims=True) + jnp.exp2(sink - m)
        p = (p * (1.0 / denom)).astype(BF16)
        vv = vd_ref[key_rows, g * LANES:(g + 1) * LANES]
        o = lax.dot_general(p, vv, (((0,), (0,)), ((), ())), preferred_element_type=F32)
        for j in range(2):
            a = o[(2 * j) * BLOCK:(2 * j + 1) * BLOCK]
            b = o[(2 * j + 1) * BLOCK:(2 * j + 2) * BLOCK]
            left = _half_lane_mask(a.shape)
            o_ref[rows, (2 * g + j) * LANES:(2 * g + j + 1) * LANES] = jnp.where(left, a, b).astype(BF16)

    def run(units):
        s_next = scores(*units[0])
        for k, (n, g, key_rows, _) in enumerate(units):
            s = s_next
            if k + 1 < len(units):
                s_next = scores(*units[k + 1])
            finish(n, g, key_rows, s)

    def with_previous(n):
        start = pl.multiple_of(i * SWA_TILE + (n - 1) * BLOCK, BLOCK)
        return [(n, g, pl.ds(start, 2 * BLOCK), slice(0, 2 * BLOCK)) for g in range(SWA_KV_HEADS)]

    later_blocks = [u for n in range(1, blocks) for u in with_previous(n)]

    @pl.when(i == 0)
    def _():
        run([(0, g, slice(0, BLOCK), slice(BLOCK, 2 * BLOCK)) for g in range(SWA_KV_HEADS)] + later_blocks)

    @pl.when(i > 0)
    def _():
        run(with_previous(0) + later_blocks)


def _swa_attention(qa, kd, vd, tbl, sink):
    b, s, _ = qa.shape
    tq = SWA_TILE
    return pl.pallas_call(
        _swa_kernel,
        grid=(b, s // tq),
        in_specs=[pl.BlockSpec((None, tq, SWA_Q_W), lambda bi, i: (bi, i, 0)),
                  pl.BlockSpec((None, s, 2 * SWA_KV_W), lambda bi, i: (bi, 0, 0)),
                  pl.BlockSpec((None, s, 2 * SWA_KV_W), lambda bi, i: (bi, 0, 0)),
                  pl.BlockSpec(tbl.shape, lambda bi, i: (0, 0, 0)),
                  pl.BlockSpec(sink.shape, lambda bi, i: (0, 0, 0))],
        out_specs=pl.BlockSpec((None, tq, SWA_Q_W), lambda bi, i: (bi, i, 0)),
        out_shape=jax.ShapeDtypeStruct((b, s, SWA_Q_W), BF16),
        compiler_params=pltpu.CompilerParams(dimension_semantics=("parallel", "parallel"),
                                             vmem_limit_bytes=VMEM_LIMIT),
        name="swa_attention",
    )(qa, kd, vd, tbl, sink)


def _diff_query_tile(i, lam, qt_ref, k_ref, vt_ref, tbl_ref, gain_ref, o_ref, m_sc, l_sc, acc_sc, s_sc, smax_sc,
                     q_sc, lam_init):
    tq = ATTN_TILE
    i_next = jnp.minimum(i + 1, qt_ref.shape[0] - 1)

    def split_components(qt):
        top = lax.broadcasted_iota(jnp.int32, qt.shape, 0) < HEAD_DIM
        zero = jnp.zeros_like(qt)
        return jnp.where(top, qt, zero), jnp.where(top, zero, qt)

    for c, (q_now, q_next) in enumerate(zip(split_components(qt_ref[i]), split_components(qt_ref[i_next]))):
        q_sc[0, c] = q_now
        q_sc[1, c] = q_next

    def prefetch_scores(j, c, for_next_tile):
        kc = k_ref[pl.ds(pl.multiple_of(j * tq, tq), tq), :]
        s = _dot(kc, q_sc[for_next_tile, c])
        s_sc[c] = s
        smax_sc[c] = jnp.max(s, axis=0, keepdims=True)

    def biased_tile(c, kind):
        s = s_sc[c]
        nq = tq // BLOCK
        blk = lambda kb, qb: s[kb * BLOCK:(kb + 1) * BLOCK, qb * BLOCK:(qb + 1) * BLOCK]
        band, diag = tbl_ref[:BLOCK, :], tbl_ref[BLOCK:, :]
        if kind == DIAGONAL:
            return [[blk(kb, qb) + diag if kb == qb else blk(kb, qb) + band if kb == qb - 1 else blk(kb, qb)
                     for kb in range(qb + 1)] for qb in range(nq)]
        return [[blk(kb, qb) + band if (kb, qb) == (nq - 1, 0) else blk(kb, qb) for kb in range(nq)]
                for qb in range(nq)]

    def softmax(c, kind, first):
        if kind == FAR:
            s = s_sc[c]
            m_new = smax_sc[c]
        else:
            cols = biased_tile(c, kind)
            m_new = jnp.concatenate(
                [functools.reduce(jnp.maximum, [jnp.max(b, axis=0, keepdims=True) for b in col]) for col in cols],
                axis=1)
        alpha = None
        if not first:
            m_prev = m_sc[c]
            m_new = jnp.maximum(m_prev, m_new)
            alpha = jnp.exp2(m_prev - m_new)
        if kind == FAR:
            p = jnp.exp2(s - m_new)
            return m_new, alpha, p, jnp.sum(p, axis=0, keepdims=True)
        p_cols, sums = [], []
        for qb, col in enumerate(cols):
            m_col = m_new[:, qb * BLOCK:(qb + 1) * BLOCK]
            ps = [jnp.exp2(b - m_col) for b in col]
            sums.append(sum(jnp.sum(p, axis=0, keepdims=True) for p in ps))
            masked = [jnp.zeros((BLOCK, BLOCK), F32)] * (tq // BLOCK - len(col))
            p_cols.append(jnp.concatenate(ps + masked, axis=0))
        return m_new, alpha, jnp.concatenate(p_cols, axis=1), jnp.concatenate(sums, axis=1)

    def step(j, kind, first):
        vct = vt_ref[j]
        last = (j == 0).astype(jnp.int32)
        j_next = jnp.where(j == 0, i_next, j - 1)
        for c in range(2):
            m_new, alpha, p, psum = softmax(c, kind, first)
            prefetch_scores(j_next, c, last)
            pv = _dot(vct, p.astype(BF16))
            m_sc[c] = m_new
            if first:
                l_sc[c] = psum
                acc_sc[c] = pv
            else:
                l_sc[c] = alpha * l_sc[c] + psum
                acc_sc[c] = alpha * acc_sc[c] + pv

    @pl.when(i == 0)
    def _():
        for c in range(2):
            prefetch_scores(i, c, 0)
        step(i, DIAGONAL, True)

    @pl.when(i > 0)
    def _():
        step(i, DIAGONAL, True)
        step(i - 1, PREVIOUS, False)

    n_far = jnp.maximum(i - 1, 0)
    half = FAR_UNROLL // 2
    n_single = n_far & (half - 1)
    n_half = n_far & half

    def far_steps(j, count):
        for u in range(count):
            step(j - u, FAR, False)

    def far_single(t, carry):
        far_steps(i - 2 - t, 1)
        return carry

    lax.fori_loop(0, n_single, far_single, 0)

    @pl.when(n_half > 0)
    def _():
        far_steps(i - 2 - n_single, half)

    def far_group(t, carry):
        far_steps(i - 2 - n_single - n_half - FAR_UNROLL * t, FAR_UNROLL)
        return carry

    lax.fori_loop(0, lax.shift_right_logical(n_far, FAR_UNROLL.bit_length() - 1), far_group, 0)

    o = acc_sc[0] / l_sc[0] - lam * (acc_sc[1] / l_sc[1])
    y = o * lax.rsqrt(jnp.mean(o * o, axis=0, keepdims=True) + EPS) * (gain_ref[...] * (1.0 - lam_init))
    o_ref[pl.ds(pl.multiple_of(i * tq, tq), tq), :] = y.astype(BF16).T


def _diff_kernel(lq1_ref, lk1_ref, lq2_ref, lk2_ref, qt_ref, k_ref, vt_ref, tbl_ref, gain_ref, o_ref,
                 m_sc, l_sc, acc_sc, s_sc, smax_sc, q_sc, *, lam_init):
    lam = (jnp.exp(jnp.sum(lq1_ref[...] * lk1_ref[...], axis=-1, keepdims=True))
           - jnp.exp(jnp.sum(lq2_ref[...] * lk2_ref[...], axis=-1, keepdims=True)) + lam_init)

    def query_tile(i, carry):
        _diff_query_tile(i, lam, qt_ref, k_ref, vt_ref, tbl_ref, gain_ref, o_ref, m_sc, l_sc, acc_sc, s_sc,
                         smax_sc, q_sc, lam_init)
        return carry

    lax.fori_loop(0, qt_ref.shape[0], query_tile, 0)


def _diff_attention(lam_vecs, qbt, kb, vbt, tbl, gain, lam_init):
    b, s, _ = kb.shape
    tq = ATTN_TILE
    n = s // tq
    vec = pl.BlockSpec((1, HEAD_DIM), lambda bi, h: (0, 0))
    return pl.pallas_call(
        functools.partial(_diff_kernel, lam_init=lam_init),
        grid=(b, DIFF_HEADS),
        in_specs=[vec, vec, vec, vec,
                  pl.BlockSpec((None, n, LANES, tq), lambda bi, h: (bi, 0, h, 0)),
                  pl.BlockSpec((None, s, LANES), lambda bi, h: (bi, 0, h)),
                  pl.BlockSpec((None, n, LANES, tq), lambda bi, h: (bi, 0, h, 0)),
                  pl.BlockSpec((None, 2 * BLOCK, BLOCK), lambda bi, h: (h, 0, 0)),
                  pl.BlockSpec((DIFF_V_DIM, 1), lambda bi, h: (0, 0))],
        out_specs=pl.BlockSpec((None, s, LANES), lambda bi, h: (bi, 0, h)),
        out_shape=jax.ShapeDtypeStruct((b, s, DIFF_V_W), BF16),
        scratch_shapes=[pltpu.VMEM((2, 1, tq), F32), pltpu.VMEM((2, 1, tq), F32),
                        pltpu.VMEM((2, DIFF_V_DIM, tq), F32), pltpu.VMEM((2, tq, tq), F32),
                        pltpu.VMEM((2, 1, tq), F32), pltpu.VMEM((2, 2, 2 * HEAD_DIM, tq), BF16)],
        compiler_params=pltpu.CompilerParams(dimension_semantics=("parallel", "parallel"),
                                             vmem_limit_bytes=VMEM_LIMIT),
        name="diff_attention",
    )(*lam_vecs, qbt, kb, vbt, tbl, gain)


def _merge_kernel(x_ref, ya_ref, yb_ref, g_ref, wg_ref, wa_ref, wb_ref, wo_ref, o_ref):
    x = x_ref[...]
    h = (x * lax.rsqrt(jnp.mean(x * x, axis=-1, keepdims=True) + EPS) * g_ref[...]).astype(BF16)
    ga = jax.nn.sigmoid(_dot(h, wg_ref[:, :D_MODEL]))
    mixed = ga * _dot(ya_ref[...], wa_ref[...])
    gb = jax.nn.sigmoid(_dot(h, wg_ref[:, D_MODEL:]))
    mixed = mixed + gb * _dot(yb_ref[...], wb_ref[...])
    o_ref[...] = x + _dot(mixed.astype(BF16), wo_ref[...])


def _merge(x2, ya, yb, g_mix, w_gate, wa, wb, wo):
    t = x2.shape[0]
    tm = ROW_TILE
    row = lambda w: pl.BlockSpec((tm, w), lambda i: (i, 0))
    full = lambda a: pl.BlockSpec(a.shape, lambda i: (0, 0), pipeline_mode=pl.Buffered(1))
    return pl.pallas_call(
        _merge_kernel,
        grid=(t // tm,),
        in_specs=[row(D_MODEL), row(SWA_Q_W), row(DIFF_V_W), full(g_mix), full(w_gate), full(wa), full(wb),
                  full(wo)],
        out_specs=row(D_MODEL),
        out_shape=jax.ShapeDtypeStruct((t, D_MODEL), F32),
        compiler_params=pltpu.CompilerParams(dimension_semantics=("parallel",),
                                             vmem_limit_bytes=VMEM_LIMIT),
        name="merge",
    )(x2, ya, yb, g_mix, w_gate, wa, wb, wo)


def _conv_mlp_kernel(x_ref, halo_ref, g_ref, wup_ref, cw_ref, cb_ref, wdn_ref, o_ref, u_sc, acc_sc):
    tm = MLP_TILE
    n_chunks = D_FF // FF_CHUNK
    first = pl.program_id(1) == 0

    def normed(x):
        return (x * lax.rsqrt(jnp.mean(x * x, axis=-1, keepdims=True) + EPS) * g_ref[...]).astype(BF16)

    x = x_ref[...]
    h = normed(x)
    h_halo = normed(halo_ref[...])
    keep = jnp.where(first, 0.0, 1.0)

    def columns(c, part):
        return slice(part * D_FF + c * FF_CHUNK, part * D_FF + (c + 1) * FF_CHUNK)

    def up(c):
        for part in range(2):
            cols = columns(c, part)
            u_sc[c % 2, part, 0:SUBLANES, :] = _dot(h_halo, wup_ref[:, cols]) * keep
            u_sc[c % 2, part, SUBLANES:, :] = _dot(h, wup_ref[:, cols])

    def conv(c, part):
        cols = columns(c, part)
        w = cw_ref[:, cols]
        u = u_sc.at[c % 2, part]
        y = (u[SUBLANES - 2:SUBLANES - 2 + tm, :] * w[0:1]
             + u[SUBLANES - 1:SUBLANES - 1 + tm, :] * w[1:2]
             + u[SUBLANES:SUBLANES + tm, :] * w[2:3])
        return y + cb_ref[:, cols]

    up(0)
    for c in range(n_chunks):
        if c + 1 < n_chunks:
            up(c + 1)
        half_g = 0.5 * conv(c, 0)
        act = ((half_g + half_g * jnp.tanh(half_g)) * conv(c, 1)).astype(BF16)
        contrib = _dot(act, wdn_ref[columns(c, 0), :])
        if c + 1 == n_chunks:
            o_ref[...] = x + acc_sc[...] + contrib
        elif c == 0:
            acc_sc[...] = contrib
        else:
            acc_sc[...] += contrib


def _conv_mlp(x3, g_ffn, w_up, conv_w, conv_b, w_down):
    b, s, _ = x3.shape
    tm = MLP_TILE
    halo_blocks = tm // SUBLANES
    full = lambda a: pl.BlockSpec(a.shape, lambda bi, i: (0,) * a.ndim, pipeline_mode=pl.Buffered(1))
    return pl.pallas_call(
        _conv_mlp_kernel,
        grid=(b, s // tm),
        in_specs=[pl.BlockSpec((None, tm, D_MODEL), lambda bi, i: (bi, i, 0)),
                  pl.BlockSpec((None, SUBLANES, D_MODEL),
                               lambda bi, i: (bi, jnp.maximum(i * halo_blocks - 1, 0), 0)),
                  full(g_ffn), full(w_up), full(conv_w), full(conv_b), full(w_down)],
        out_specs=pl.BlockSpec((None, tm, D_MODEL), lambda bi, i: (bi, i, 0)),
        out_shape=jax.ShapeDtypeStruct((b, s, D_MODEL), F32),
        scratch_shapes=[pltpu.VMEM((2, 2, tm + SUBLANES, FF_CHUNK), F32),
                        pltpu.VMEM((tm, D_MODEL), F32)],
        compiler_params=pltpu.CompilerParams(dimension_semantics=("parallel", "parallel"),
                                             vmem_limit_bytes=VMEM_LIMIT),
        name="conv_mlp",
    )(x3, x3, g_ffn, w_up, conv_w, conv_b, w_down)


def _swa_bucket_idx():
    rel = BLOCK + np.arange(BLOCK)[None, :] - np.arange(2 * BLOCK)[:, None]
    return np.where((rel >= 0) & (rel < WINDOW), _t5_bucket_np(rel), -1).astype(np.int32)


def _diff_bucket_idx():
    rel = np.arange(BLOCK)[None, :] - np.arange(BLOCK)[:, None]
    below = _t5_bucket_np(rel + BLOCK)
    diagonal = np.where(rel >= 0, _t5_bucket_np(rel), -1)
    return np.concatenate([below, diagonal]).astype(np.int32)


def _layer(x, l, swa_tbl, diff_tbl, g_mix, w_in, qn_a, kn_a, sinks, qn_b, kn_b, lam_q1, lam_k1,
           lam_q2, lam_k2, subln_b, w_br_a, w_br_b, w_o, g_ffn, w_up, conv_w, conv_b, w_down):
    b, s, _ = x.shape
    t = b * s
    lam_init = 0.8 - 0.6 * math.exp(-0.3 * l)
    q_scale = HEAD_DIM ** -0.5 * LOG2E
    tile = lambda v, reps, scale=1.0: (jnp.tile(v.astype(F32), reps) * scale).reshape(1, -1)

    x2 = x.reshape(t, D_MODEL)
    w_in_bf = w_in[l].astype(BF16)
    g_mix_row = g_mix[l].reshape(1, -1)
    qa, kd, vd, qbt, kb, vbt = _in_proj(
        x2, g_mix_row, w_in_bf[:, :IN_WIDTH - GATE_W],
        tile(qn_a[l], SWA_Q_HEADS, q_scale), tile(kn_a[l], SWA_KV_HEADS),
        tile(qn_b[l], 2 * DIFF_HEADS, q_scale), tile(kn_b[l], 2 * DIFF_HEADS))

    sink = jnp.repeat(sinks[l].astype(F32) * LOG2E, BLOCK).reshape(SWA_KV_HEADS, 1, SWA_GROUP * BLOCK)
    ya = _swa_attention(qa.reshape(b, s, -1), kd.reshape(b, s, -1), vd.reshape(b, s, -1), swa_tbl, sink)

    lam_vecs = [v[l].astype(F32).reshape(1, HEAD_DIM) for v in (lam_q1, lam_k1, lam_q2, lam_k2)]
    chunks = s // ATTN_TILE
    yb = _diff_attention(lam_vecs, qbt.reshape(b, chunks, DIFF_QK_W, ATTN_TILE), kb.reshape(b, s, -1),
                         vbt.reshape(b, chunks, DIFF_V_W, ATTN_TILE), diff_tbl,
                         subln_b[l].astype(F32).reshape(-1, 1), lam_init)

    x2 = _merge(x2, ya.reshape(t, -1), yb.reshape(t, -1), g_mix_row, w_in_bf[:, IN_WIDTH - GATE_W:],
                w_br_a[l].astype(BF16), w_br_b[l].astype(BF16), w_o[l].astype(BF16))
    x3 = _conv_mlp(x2.reshape(b, s, D_MODEL), g_ffn[l].reshape(1, -1), w_up[l].astype(BF16),
                   conv_w[l].astype(F32), conv_b[l].astype(F32).reshape(1, -1), w_down[l].astype(BF16))
    return x3


def kernel(x, rel_bias, g_mix, w_in, qn_a, kn_a, sinks, qn_b, kn_b, lam_q1, lam_k1, lam_q2, lam_k2, subln_b,
           w_br_a, w_br_b, w_o, g_ffn, w_up, conv_w, conv_b, w_down):
    rb = rel_bias.astype(F32)
    swa_tbl = _bias_tables(rb, _swa_bucket_idx(), SWA_Q_HEADS, 0, False)
    diff_tbl = _bias_tables(rb, _diff_bucket_idx(), DIFF_HEADS, SWA_Q_HEADS, True)
    for l in range(g_mix.shape[0]):
        x = _layer(x, l, swa_tbl, diff_tbl, g_mix, w_in, qn_a, kn_a, sinks, qn_b, kn_b, lam_q1, lam_k1,
                   lam_q2, lam_k2, subln_b, w_br_a, w_br_b, w_o, g_ffn, w_up, conv_w, conv_b, w_down)
    return x
```

```python
import functools
import math

import numpy as np
import jax
import jax.numpy as jnp
from jax import lax
from jax.experimental import pallas as pl
from jax.experimental.pallas import tpu as pltpu

D_MODEL = 1024
HEAD_DIM = 64
SWA_Q_HEADS = 8
SWA_KV_HEADS = 2
SWA_GROUP = SWA_Q_HEADS // SWA_KV_HEADS
WINDOW = 128
BLOCK = 128
DIFF_HEADS = 4
DIFF_V_DIM = 2 * HEAD_DIM
N_BUCKETS = 32
MAX_DISTANCE = 128
D_FF = 2816
CONV_WIDTH = 3
EPS = 1e-6

SWA_Q_W = SWA_Q_HEADS * HEAD_DIM
SWA_KV_W = SWA_KV_HEADS * HEAD_DIM
DIFF_QK_W = DIFF_HEADS * 2 * HEAD_DIM
DIFF_V_W = DIFF_HEADS * DIFF_V_DIM
GATE_W = 2 * D_MODEL
IN_WIDTH = SWA_Q_W + 2 * SWA_KV_W + 2 * DIFF_QK_W + DIFF_V_W + GATE_W

LANES = 128
SUBLANES = 8
LOG2E = math.log2(math.e)
NEG_INF = float("-inf")

ROW_TILE = 1024
MLP_TILE = 512
SWA_TILE = 1024
ATTN_TILE = 512
FF_CHUNK = 256
FAR_UNROLL = 8
VMEM_LIMIT = 56 * 1024 * 1024

F32 = jnp.float32
BF16 = jnp.bfloat16

DIAGONAL, PREVIOUS, FAR = "diagonal", "previous", "far"


def _t5_bucket_np(rel):
    n = np.maximum(rel, 0)
    max_exact = N_BUCKETS // 2
    nf = np.maximum(n, 1).astype(np.float64)
    large = max_exact + (np.log(nf / max_exact) / math.log(MAX_DISTANCE / max_exact)
                         * (N_BUCKETS - max_exact)).astype(np.int32)
    large = np.minimum(large, N_BUCKETS - 1)
    return np.where(n < max_exact, n, large).astype(np.int32)


def _dot(a, b):
    return jnp.dot(a, b, preferred_element_type=F32)


def _dot_nt(a, b):
    return lax.dot_general(a, b, (((1,), (1,)), ((), ())), preferred_element_type=F32)


def _half_lane_mask(shape):
    return lax.broadcasted_iota(jnp.int32, shape, len(shape) - 1) < HEAD_DIM


def _bias_table_kernel(rb_ref, idx_ref, out_ref, *, col0, shift_far):
    h = pl.program_id(0) + col0
    idx = idx_ref[...]
    shift = rb_ref[N_BUCKETS - 1, h] if shift_far else 0.0
    acc = jnp.full(idx.shape, NEG_INF, F32)
    for b in range(N_BUCKETS):
        acc = jnp.where(idx == b, (rb_ref[b, h] - shift) * LOG2E, acc)
    out_ref[...] = acc


def _bias_tables(rel_bias, idx, n_heads, col0, shift_far):
    r, c = idx.shape
    return pl.pallas_call(
        functools.partial(_bias_table_kernel, col0=col0, shift_far=shift_far),
        grid=(n_heads,),
        in_specs=[pl.BlockSpec(memory_space=pltpu.SMEM),
                  pl.BlockSpec((r, c), lambda h: (0, 0))],
        out_specs=pl.BlockSpec((None, r, c), lambda h: (h, 0, 0)),
        out_shape=jax.ShapeDtypeStruct((n_heads, r, c), F32),
        name="bias_tables",
    )(rel_bias, jnp.asarray(idx))


def _pair_rms_norm(z, gain):
    outs = []
    for j in range(z.shape[1] // LANES):
        s = z[:, j * LANES:(j + 1) * LANES]
        left = _half_lane_mask(s.shape)
        sq = s * s
        tot = jnp.sum(sq, axis=-1, keepdims=True)
        lo = jnp.sum(jnp.where(left, sq, 0.0), axis=-1, keepdims=True)
        ms = jnp.where(left, lo, tot - lo) * (1.0 / HEAD_DIM)
        outs.append(s * lax.rsqrt(ms + EPS))
    return jnp.concatenate(outs, axis=-1) * gain


def _dup_halves(z):
    left = _half_lane_mask(z.shape)
    rolled = pltpu.roll(z, HEAD_DIM, axis=1)
    return jnp.concatenate([jnp.where(left, z, rolled), jnp.where(left, rolled, z)], axis=-1)


def _in_proj_kernel(x_ref, g_ref, w_ref, gqa_ref, gka_ref, gqb_ref, gkb_ref,
                    qa_ref, kd_ref, vd_ref, qbt_ref, kb_ref, vbt_ref):
    x = x_ref[...]
    h = (x * lax.rsqrt(jnp.mean(x * x, axis=-1, keepdims=True) + EPS) * g_ref[...]).astype(BF16)

    widths = (SWA_Q_W, SWA_KV_W, SWA_KV_W, DIFF_QK_W, DIFF_QK_W, DIFF_V_W)
    starts = dict(zip(("qa", "ka", "va", "qb", "kb", "vb"), np.cumsum((0,) + widths[:-1])))

    def proj(name, width):
        return _dot(h, w_ref[:, starts[name]:starts[name] + width])

    z_kb = proj("kb", DIFF_QK_W)
    z_qb = proj("qb", DIFF_QK_W)
    kb_ref[...] = _pair_rms_norm(z_kb, gkb_ref[...]).astype(BF16)
    z_qa = proj("qa", SWA_Q_W)
    qbt_ref[...] = _pair_rms_norm(z_qb, gqb_ref[...]).astype(BF16).T
    z_ka = proj("ka", SWA_KV_W)
    qa_ref[...] = _pair_rms_norm(z_qa, gqa_ref[...]).astype(BF16)
    z_vb = proj("vb", DIFF_V_W)
    kd_ref[...] = _dup_halves(_pair_rms_norm(z_ka, gka_ref[...])).astype(BF16)
    z_va = proj("va", SWA_KV_W)
    vbt_ref[...] = z_vb.astype(BF16).T
    vd_ref[...] = _dup_halves(z_va).astype(BF16)


def _in_proj(x2, g_mix, w_in, gqa, gka, gqb, gkb):
    t = x2.shape[0]
    tm = ATTN_TILE
    row = lambda w: pl.BlockSpec((tm, w), lambda i: (i, 0))
    full = lambda a: pl.BlockSpec(a.shape, lambda i: (0, 0), pipeline_mode=pl.Buffered(1))
    tposed = lambda w: pl.BlockSpec((None, w, tm), lambda i: (i, 0, 0))
    bf = lambda w: jax.ShapeDtypeStruct((t, w), BF16)
    bft = lambda w: jax.ShapeDtypeStruct((t // tm, w, tm), BF16)
    return pl.pallas_call(
        _in_proj_kernel,
        grid=(t // tm,),
        in_specs=[row(D_MODEL), full(g_mix), full(w_in), full(gqa), full(gka), full(gqb), full(gkb)],
        out_specs=[row(SWA_Q_W), row(2 * SWA_KV_W), row(2 * SWA_KV_W), tposed(DIFF_QK_W), row(DIFF_QK_W),
                   tposed(DIFF_V_W)],
        out_shape=[bf(SWA_Q_W), bf(2 * SWA_KV_W), bf(2 * SWA_KV_W), bft(DIFF_QK_W), bf(DIFF_QK_W),
                   bft(DIFF_V_W)],
        compiler_params=pltpu.CompilerParams(dimension_semantics=("parallel",),
                                             vmem_limit_bytes=VMEM_LIMIT),
        name="in_proj",
    )(x2, g_mix, w_in, gqa, gka, gqb, gkb)


def _swa_kernel(q_ref, kd_ref, vd_ref, tbl_ref, sink_ref, o_ref):
    i = pl.program_id(1)
    blocks = SWA_TILE // BLOCK

    def scores(n, g, key_rows, table_rows):
        rows = slice(n * BLOCK, (n + 1) * BLOCK)
        stacked = []
        for j in range(2):
            qp = q_ref[rows, (2 * g + j) * LANES:(2 * g + j + 1) * LANES]
            left = _half_lane_mask(qp.shape)
            zero = jnp.zeros_like(qp)
            stacked += [jnp.where(left, qp, zero), jnp.where(left, zero, qp)]
        kk = kd_ref[key_rows, g * LANES:(g + 1) * LANES]
        bias = jnp.concatenate([tbl_ref[SWA_GROUP * g + j, table_rows, :] for j in range(SWA_GROUP)], axis=1)
        return _dot_nt(kk, jnp.concatenate(stacked, axis=0)) + bias

    def finish(n, g, key_rows, s):
        rows = slice(n * BLOCK, (n + 1) * BLOCK)
        sink = sink_ref[g]
        m = jnp.maximum(jnp.max(s, axis=0, keepdims=True), sink)
        p = jnp.exp2(s - m)
        denom = jnp.sum(p, axis=0, keepdims=True) + jnp.exp2(sink - m)
        p = (p * (1.0 / denom)).astype(BF16)
        vv = vd_ref[key_rows, g * LANES:(g + 1) * LANES]
        o = lax.dot_general(p, vv, (((0,), (0,)), ((), ())), preferred_element_type=F32)
        for j in range(2):
            a = o[(2 * j) * BLOCK:(2 * j + 1) * BLOCK]
            b = o[(2 * j + 1) * BLOCK:(2 * j + 2) * BLOCK]
            left = _half_lane_mask(a.shape)
            o_ref[rows, (2 * g + j) * LANES:(2 * g + j + 1) * LANES] = jnp.where(left, a, b).astype(BF16)

    def run(units):
        s_next = scores(*units[0])
        for k, (n, g, key_rows, _) in enumerate(units):
            s = s_next
            if k + 1 < len(units):
                s_next = scores(*units[k + 1])
            finish(n, g, key_rows, s)

    def with_previous(n):
        start = pl.multiple_of(i * SWA_TILE + (n - 1) * BLOCK, BLOCK)
        return [(n, g, pl.ds(start, 2 * BLOCK), slice(0, 2 * BLOCK)) for g in range(SWA_KV_HEADS)]

    later_blocks = [u for n in range(1, blocks) for u in with_previous(n)]

    @pl.when(i == 0)
    def _():
        run([(0, g, slice(0, BLOCK), slice(BLOCK, 2 * BLOCK)) for g in range(SWA_KV_HEADS)] + later_blocks)

    @pl.when(i > 0)
    def _():
        run(with_previous(0) + later_blocks)


def _swa_attention(qa, kd, vd, tbl, sink):
    b, s, _ = qa.shape
    tq = SWA_TILE
    return pl.pallas_call(
        _swa_kernel,
        grid=(b, s // tq),
        in_specs=[pl.BlockSpec((None, tq, SWA_Q_W), lambda bi, i: (bi, i, 0)),
                  pl.BlockSpec((None, s, 2 * SWA_KV_W), lambda bi, i: (bi, 0, 0)),
                  pl.BlockSpec((None, s, 2 * SWA_KV_W), lambda bi, i: (bi, 0, 0)),
                  pl.BlockSpec(tbl.shape, lambda bi, i: (0, 0, 0)),
                  pl.BlockSpec(sink.shape, lambda bi, i: (0, 0, 0))],
        out_specs=pl.BlockSpec((None, tq, SWA_Q_W), lambda bi, i: (bi, i, 0)),
        out_shape=jax.ShapeDtypeStruct((b, s, SWA_Q_W), BF16),
        compiler_params=pltpu.CompilerParams(dimension_semantics=("parallel", "parallel"),
                                             vmem_limit_bytes=VMEM_LIMIT),
        name="swa_attention",
    )(qa, kd, vd, tbl, sink)


def _diff_query_tile(i, lam, qt_ref, k_ref, vt_ref, tbl_ref, gain_ref, o_ref, m_sc, l_sc, acc_sc, s_sc, smax_sc,
                     lam_init):
    tq = ATTN_TILE
    qt = qt_ref[i]
    top = lax.broadcasted_iota(jnp.int32, qt.shape, 0) < HEAD_DIM
    zero = jnp.zeros_like(qt)
    qts = (jnp.where(top, qt, zero), jnp.where(top, zero, qt))

    def write_output(tile):
        o = acc_sc[0] / l_sc[0] - lam * (acc_sc[1] / l_sc[1])
        y = o * lax.rsqrt(jnp.mean(o * o, axis=0, keepdims=True) + EPS) * (gain_ref[...] * (1.0 - lam_init))
        o_ref[pl.ds(pl.multiple_of(tile * tq, tq), tq), :] = y.astype(BF16).T

    def prefetch_scores(j, c):
        kc = k_ref[pl.ds(pl.multiple_of(j * tq, tq), tq), :]
        s = _dot(kc, qts[c])
        s_sc[c] = s
        smax_sc[c] = jnp.max(s, axis=0, keepdims=True)

    def biased_tile(c, kind):
        s = s_sc[c]
        nq = tq // BLOCK
        blk = lambda kb, qb: s[kb * BLOCK:(kb + 1) * BLOCK, qb * BLOCK:(qb + 1) * BLOCK]
        band, diag = tbl_ref[:BLOCK, :], tbl_ref[BLOCK:, :]
        if kind == DIAGONAL:
            return [[blk(kb, qb) + diag if kb == qb else blk(kb, qb) + band if kb == qb - 1 else blk(kb, qb)
                     for kb in range(qb + 1)] for qb in range(nq)]
        return [[blk(kb, qb) + band if (kb, qb) == (nq - 1, 0) else blk(kb, qb) for kb in range(nq)]
                for qb in range(nq)]

    def softmax(c, kind, first):
        if kind == FAR:
            s = s_sc[c]
            m_new = smax_sc[c]
        else:
            cols = biased_tile(c, kind)
            m_new = jnp.concatenate(
                [functools.reduce(jnp.maximum, [jnp.max(b, axis=0, keepdims=True) for b in col]) for col in cols],
                axis=1)
        alpha = None
        if not first:
            m_prev = m_sc[c]
            m_new = jnp.maximum(m_prev, m_new)
            alpha = jnp.exp2(m_prev - m_new)
        if kind == FAR:
            p = jnp.exp2(s - m_new)
            return m_new, alpha, p, jnp.sum(p, axis=0, keepdims=True)
        p_cols, sums = [], []
        for qb, col in enumerate(cols):
            m_col = m_new[:, qb * BLOCK:(qb + 1) * BLOCK]
            ps = [jnp.exp2(b - m_col) for b in col]
            sums.append(sum(jnp.sum(p, axis=0, keepdims=True) for p in ps))
            masked = [jnp.zeros((BLOCK, BLOCK), F32)] * (tq // BLOCK - len(col))
            p_cols.append(jnp.concatenate(ps + masked, axis=0))
        return m_new, alpha, jnp.concatenate(p_cols, axis=1), jnp.concatenate(sums, axis=1)

    def step(j, j_next, kind, first):
        vct = vt_ref[j]
        for c in range(2):
            m_new, alpha, p, psum = softmax(c, kind, first)
            prefetch_scores(j_next, c)
            pv = _dot(vct, p.astype(BF16))
            m_sc[c] = m_new
            if first:
                l_sc[c] = psum
                acc_sc[c] = pv
            else:
                l_sc[c] = alpha * l_sc[c] + psum
                acc_sc[c] = alpha * acc_sc[c] + pv

    def diagonal_step(j_next, finish_previous):
        for c in range(2):
            prefetch_scores(i, c)
        if finish_previous:
            write_output(i - 1)
        step(i, j_next, DIAGONAL, True)

    @pl.when(i == 0)
    def _():
        diagonal_step(0, False)

    @pl.when(i > 0)
    def _():
        diagonal_step(i - 1, True)
        step(i - 1, jnp.maximum(i - 2, 0), PREVIOUS, False)

    n_far = jnp.maximum(i - 1, 0)
    half = FAR_UNROLL // 2
    n_single = n_far & (half - 1)
    n_half = n_far & half

    def far_steps(j, count):
        for u in range(count):
            step(j - u, jnp.maximum(j - u - 1, 0), FAR, False)

    def far_single(t, carry):
        far_steps(i - 2 - t, 1)
        return carry

    lax.fori_loop(0, n_single, far_single, 0)

    @pl.when(n_half > 0)
    def _():
        far_steps(i - 2 - n_single, half)

    def far_group(t, carry):
        far_steps(i - 2 - n_single - n_half - FAR_UNROLL * t, FAR_UNROLL)
        return carry

    lax.fori_loop(0, lax.shift_right_logical(n_far, FAR_UNROLL.bit_length() - 1), far_group, 0)

    @pl.when(i == qt_ref.shape[0] - 1)
    def _():
        write_output(i)


def _diff_kernel(lq1_ref, lk1_ref, lq2_ref, lk2_ref, qt_ref, k_ref, vt_ref, tbl_ref, gain_ref, o_ref,
                 m_sc, l_sc, acc_sc, s_sc, smax_sc, *, lam_init):
    lam = (jnp.exp(jnp.sum(lq1_ref[...] * lk1_ref[...], axis=-1, keepdims=True))
           - jnp.exp(jnp.sum(lq2_ref[...] * lk2_ref[...], axis=-1, keepdims=True)) + lam_init)

    def query_tile(i, carry):
        _diff_query_tile(i, lam, qt_ref, k_ref, vt_ref, tbl_ref, gain_ref, o_ref, m_sc, l_sc, acc_sc, s_sc,
                         smax_sc, lam_init)
        return carry

    lax.fori_loop(0, qt_ref.shape[0], query_tile, 0)


def _diff_attention(lam_vecs, qbt, kb, vbt, tbl, gain, lam_init):
    b, s, _ = kb.shape
    tq = ATTN_TILE
    n = s // tq
    vec = pl.BlockSpec((1, HEAD_DIM), lambda bi, h: (0, 0))
    return pl.pallas_call(
        functools.partial(_diff_kernel, lam_init=lam_init),
        grid=(b, DIFF_HEADS),
        in_specs=[vec, vec, vec, vec,
                  pl.BlockSpec((None, n, LANES, tq), lambda bi, h: (bi, 0, h, 0)),
                  pl.BlockSpec((None, s, LANES), lambda bi, h: (bi, 0, h)),
                  pl.BlockSpec((None, n, LANES, tq), lambda bi, h: (bi, 0, h, 0)),
                  pl.BlockSpec((None, 2 * BLOCK, BLOCK), lambda bi, h: (h, 0, 0)),
                  pl.BlockSpec((DIFF_V_DIM, 1), lambda bi, h: (0, 0))],
        out_specs=pl.BlockSpec((None, s, LANES), lambda bi, h: (bi, 0, h)),
        out_shape=jax.ShapeDtypeStruct((b, s, DIFF_V_W), BF16),
        scratch_shapes=[pltpu.VMEM((2, 1, tq), F32), pltpu.VMEM((2, 1, tq), F32),
                        pltpu.VMEM((2, DIFF_V_DIM, tq), F32), pltpu.VMEM((2, tq, tq), F32),
                        pltpu.VMEM((2, 1, tq), F32)],
        compiler_params=pltpu.CompilerParams(dimension_semantics=("parallel", "parallel"),
                                             vmem_limit_bytes=VMEM_LIMIT),
        name="diff_attention",
    )(*lam_vecs, qbt, kb, vbt, tbl, gain)


def _merge_kernel(x_ref, ya_ref, yb_ref, g_ref, wg_ref, wa_ref, wb_ref, wo_ref, o_ref):
    x = x_ref[...]
    h = (x * lax.rsqrt(jnp.mean(x * x, axis=-1, keepdims=True) + EPS) * g_ref[...]).astype(BF16)
    ga = jax.nn.sigmoid(_dot(h, wg_ref[:, :D_MODEL]))
    mixed = ga * _dot(ya_ref[...], wa_ref[...])
    gb = jax.nn.sigmoid(_dot(h, wg_ref[:, D_MODEL:]))
    mixed = mixed + gb * _dot(yb_ref[...], wb_ref[...])
    o_ref[...] = x + _dot(mixed.astype(BF16), wo_ref[...])


def _merge(x2, ya, yb, g_mix, w_gate, wa, wb, wo):
    t = x2.shape[0]
    tm = ROW_TILE
    row = lambda w: pl.BlockSpec((tm, w), lambda i: (i, 0))
    full = lambda a: pl.BlockSpec(a.shape, lambda i: (0, 0), pipeline_mode=pl.Buffered(1))
    return pl.pallas_call(
        _merge_kernel,
        grid=(t // tm,),
        in_specs=[row(D_MODEL), row(SWA_Q_W), row(DIFF_V_W), full(g_mix), full(w_gate), full(wa), full(wb),
                  full(wo)],
        out_specs=row(D_MODEL),
        out_shape=jax.ShapeDtypeStruct((t, D_MODEL), F32),
        compiler_params=pltpu.CompilerParams(dimension_semantics=("parallel",),
                                             vmem_limit_bytes=VMEM_LIMIT),
        name="merge",
    )(x2, ya, yb, g_mix, w_gate, wa, wb, wo)


def _conv_mlp_kernel(x_ref, halo_ref, g_ref, wup_ref, cw_ref, cb_ref, wdn_ref, o_ref, u_sc, acc_sc):
    tm = MLP_TILE
    n_chunks = D_FF // FF_CHUNK
    first = pl.program_id(1) == 0

    def normed(x):
        return (x * lax.rsqrt(jnp.mean(x * x, axis=-1, keepdims=True) + EPS) * g_ref[...]).astype(BF16)

    x = x_ref[...]
    h = normed(x)
    h_halo = normed(halo_ref[...])
    keep = jnp.where(first, 0.0, 1.0)

    def columns(c, part):
        return slice(part * D_FF + c * FF_CHUNK, part * D_FF + (c + 1) * FF_CHUNK)

    def up(c):
        for part in range(2):
            cols = columns(c, part)
            u_sc[c % 2, part, 0:SUBLANES, :] = _dot(h_halo, wup_ref[:, cols]) * keep
            u_sc[c % 2, part, SUBLANES:, :] = _dot(h, wup_ref[:, cols])

    def conv(c, part):
        cols = columns(c, part)
        w = cw_ref[:, cols]
        u = u_sc.at[c % 2, part]
        y = (u[SUBLANES - 2:SUBLANES - 2 + tm, :] * w[0:1]
             + u[SUBLANES - 1:SUBLANES - 1 + tm, :] * w[1:2]
             + u[SUBLANES:SUBLANES + tm, :] * w[2:3])
        return y + cb_ref[:, cols]

    up(0)
    for c in range(n_chunks):
        if c + 1 < n_chunks:
            up(c + 1)
        half_g = 0.5 * conv(c, 0)
        act = ((half_g + half_g * jnp.tanh(half_g)) * conv(c, 1)).astype(BF16)
        contrib = _dot(act, wdn_ref[columns(c, 0), :])
        if c + 1 == n_chunks:
            o_ref[...] = x + acc_sc[...] + contrib
        elif c == 0:
            acc_sc[...] = contrib
        else:
            acc_sc[...] += contrib


def _conv_mlp(x3, g_ffn, w_up, conv_w, conv_b, w_down):
    b, s, _ = x3.shape
    tm = MLP_TILE
    halo_blocks = tm // SUBLANES
    full = lambda a: pl.BlockSpec(a.shape, lambda bi, i: (0,) * a.ndim, pipeline_mode=pl.Buffered(1))
    return pl.pallas_call(
        _conv_mlp_kernel,
        grid=(b, s // tm),
        in_specs=[pl.BlockSpec((None, tm, D_MODEL), lambda bi, i: (bi, i, 0)),
                  pl.BlockSpec((None, SUBLANES, D_MODEL),
                               lambda bi, i: (bi, jnp.maximum(i * halo_blocks - 1, 0), 0)),
                  full(g_ffn), full(w_up), full(conv_w), full(conv_b), full(w_down)],
        out_specs=pl.BlockSpec((None, tm, D_MODEL), lambda bi, i: (bi, i, 0)),
        out_shape=jax.ShapeDtypeStruct((b, s, D_MODEL), F32),
        scratch_shapes=[pltpu.VMEM((2, 2, tm + SUBLANES, FF_CHUNK), F32),
                        pltpu.VMEM((tm, D_MODEL), F32)],
        compiler_params=pltpu.CompilerParams(dimension_semantics=("parallel", "parallel"),
                                             vmem_limit_bytes=VMEM_LIMIT),
        name="conv_mlp",
    )(x3, x3, g_ffn, w_up, conv_w, conv_b, w_down)


def _swa_bucket_idx():
    rel = BLOCK + np.arange(BLOCK)[None, :] - np.arange(2 * BLOCK)[:, None]
    return np.where((rel >= 0) & (rel < WINDOW), _t5_bucket_np(rel), -1).astype(np.int32)


def _diff_bucket_idx():
    rel = np.arange(BLOCK)[None, :] - np.arange(BLOCK)[:, None]
    below = _t5_bucket_np(rel + BLOCK)
    diagonal = np.where(rel >= 0, _t5_bucket_np(rel), -1)
    return np.concatenate([below, diagonal]).astype(np.int32)


def _layer(x, l, swa_tbl, diff_tbl, g_mix, w_in, qn_a, kn_a, sinks, qn_b, kn_b, lam_q1, lam_k1,
           lam_q2, lam_k2, subln_b, w_br_a, w_br_b, w_o, g_ffn, w_up, conv_w, conv_b, w_down):
    b, s, _ = x.shape
    t = b * s
    lam_init = 0.8 - 0.6 * math.exp(-0.3 * l)
    q_scale = HEAD_DIM ** -0.5 * LOG2E
    tile = lambda v, reps, scale=1.0: (jnp.tile(v.astype(F32), reps) * scale).reshape(1, -1)

    x2 = x.reshape(t, D_MODEL)
    w_in_bf = w_in[l].astype(BF16)
    g_mix_row = g_mix[l].reshape(1, -1)
    qa, kd, vd, qbt, kb, vbt = _in_proj(
        x2, g_mix_row, w_in_bf[:, :IN_WIDTH - GATE_W],
        tile(qn_a[l], SWA_Q_HEADS, q_scale), tile(kn_a[l], SWA_KV_HEADS),
        tile(qn_b[l], 2 * DIFF_HEADS, q_scale), tile(kn_b[l], 2 * DIFF_HEADS))

    sink = jnp.repeat(sinks[l].astype(F32) * LOG2E, BLOCK).reshape(SWA_KV_HEADS, 1, SWA_GROUP * BLOCK)
    ya = _swa_attention(qa.reshape(b, s, -1), kd.reshape(b, s, -1), vd.reshape(b, s, -1), swa_tbl, sink)

    lam_vecs = [v[l].astype(F32).reshape(1, HEAD_DIM) for v in (lam_q1, lam_k1, lam_q2, lam_k2)]
    chunks = s // ATTN_TILE
    yb = _diff_attention(lam_vecs, qbt.reshape(b, chunks, DIFF_QK_W, ATTN_TILE), kb.reshape(b, s, -1),
                         vbt.reshape(b, chunks, DIFF_V_W, ATTN_TILE), diff_tbl,
                         subln_b[l].astype(F32).reshape(-1, 1), lam_init)

    x2 = _merge(x2, ya.reshape(t, -1), yb.reshape(t, -1), g_mix_row, w_in_bf[:, IN_WIDTH - GATE_W:],
                w_br_a[l].astype(BF16), w_br_b[l].astype(BF16), w_o[l].astype(BF16))
    x3 = _conv_mlp(x2.reshape(b, s, D_MODEL), g_ffn[l].reshape(1, -1), w_up[l].astype(BF16),
                   conv_w[l].astype(F32), conv_b[l].astype(F32).reshape(1, -1), w_down[l].astype(BF16))
    return x3


def kernel(x, rel_bias, g_mix, w_in, qn_a, kn_a, sinks, qn_b, kn_b, lam_q1, lam_k1, lam_q2, lam_k2, subln_b,
           w_br_a, w_br_b, w_o, g_ffn, w_up, conv_w, conv_b, w_down):
    rb = rel_bias.astype(F32)
    swa_tbl = _bias_tables(rb, _swa_bucket_idx(), SWA_Q_HEADS, 0, False)
    diff_tbl = _bias_tables(rb, _diff_bucket_idx(), DIFF_HEADS, SWA_Q_HEADS, True)
    for l in range(g_mix.shape[0]):
        x = _layer(x, l, swa_tbl, diff_tbl, g_mix, w_in, qn_a, kn_a, sinks, qn_b, kn_b, lam_q1, lam_k1,
                   lam_q2, lam_k2, subln_b, w_br_a, w_br_b, w_o, g_ffn, w_up, conv_w, conv_b, w_down)
    return x
```

```python
import functools
import math

import numpy as np
import jax
import jax.numpy as jnp
from jax import lax
from jax.experimental import pallas as pl
from jax.experimental.pallas import tpu as pltpu

D_MODEL = 1024
HEAD_DIM = 64
SWA_Q_HEADS = 8
SWA_KV_HEADS = 2
SWA_GROUP = SWA_Q_HEADS // SWA_KV_HEADS
WINDOW = 128
BLOCK = 128
DIFF_HEADS = 4
DIFF_V_DIM = 2 * HEAD_DIM
N_BUCKETS = 32
MAX_DISTANCE = 128
D_FF = 2816
CONV_WIDTH = 3
EPS = 1e-6

SWA_Q_W = SWA_Q_HEADS * HEAD_DIM
SWA_KV_W = SWA_KV_HEADS * HEAD_DIM
DIFF_QK_W = DIFF_HEADS * 2 * HEAD_DIM
DIFF_V_W = DIFF_HEADS * DIFF_V_DIM
GATE_W = 2 * D_MODEL
IN_WIDTH = SWA_Q_W + 2 * SWA_KV_W + 2 * DIFF_QK_W + DIFF_V_W + GATE_W

LANES = 128
SUBLANES = 8
LOG2E = math.log2(math.e)
NEG_INF = float("-inf")

ROW_TILE = 1024
MLP_TILE = 512
SWA_TILE = 1024
ATTN_TILE = 512
FF_CHUNK = 256
FAR_UNROLL = 8
VMEM_LIMIT = 56 * 1024 * 1024

F32 = jnp.float32
BF16 = jnp.bfloat16

DIAGONAL, PREVIOUS, FAR = "diagonal", "previous", "far"


def _t5_bucket_np(rel):
    n = np.maximum(rel, 0)
    max_exact = N_BUCKETS // 2
    nf = np.maximum(n, 1).astype(np.float64)
    large = max_exact + (np.log(nf / max_exact) / math.log(MAX_DISTANCE / max_exact)
                         * (N_BUCKETS - max_exact)).astype(np.int32)
    large = np.minimum(large, N_BUCKETS - 1)
    return np.where(n < max_exact, n, large).astype(np.int32)


def _dot(a, b):
    return jnp.dot(a, b, preferred_element_type=F32)


def _dot_nt(a, b):
    return lax.dot_general(a, b, (((1,), (1,)), ((), ())), preferred_element_type=F32)


def _half_lane_mask(shape):
    return lax.broadcasted_iota(jnp.int32, shape, len(shape) - 1) < HEAD_DIM


def _bias_table_kernel(rb_ref, idx_ref, out_ref, *, col0, shift_far):
    h = pl.program_id(0) + col0
    idx = idx_ref[...]
    shift = rb_ref[N_BUCKETS - 1, h] if shift_far else 0.0
    acc = jnp.full(idx.shape, NEG_INF, F32)
    for b in range(N_BUCKETS):
        acc = jnp.where(idx == b, (rb_ref[b, h] - shift) * LOG2E, acc)
    out_ref[...] = acc


def _bias_tables(rel_bias, idx, n_heads, col0, shift_far):
    r, c = idx.shape
    return pl.pallas_call(
        functools.partial(_bias_table_kernel, col0=col0, shift_far=shift_far),
        grid=(n_heads,),
        in_specs=[pl.BlockSpec(memory_space=pltpu.SMEM),
                  pl.BlockSpec((r, c), lambda h: (0, 0))],
        out_specs=pl.BlockSpec((None, r, c), lambda h: (h, 0, 0)),
        out_shape=jax.ShapeDtypeStruct((n_heads, r, c), F32),
        name="bias_tables",
    )(rel_bias, jnp.asarray(idx))


def _pair_rms_norm(z, gain):
    outs = []
    for j in range(z.shape[1] // LANES):
        s = z[:, j * LANES:(j + 1) * LANES]
        left = _half_lane_mask(s.shape)
        sq = s * s
        tot = jnp.sum(sq, axis=-1, keepdims=True)
        lo = jnp.sum(jnp.where(left, sq, 0.0), axis=-1, keepdims=True)
        ms = jnp.where(left, lo, tot - lo) * (1.0 / HEAD_DIM)
        outs.append(s * lax.rsqrt(ms + EPS))
    return jnp.concatenate(outs, axis=-1) * gain


def _dup_halves(z):
    left = _half_lane_mask(z.shape)
    rolled = pltpu.roll(z, HEAD_DIM, axis=1)
    return jnp.concatenate([jnp.where(left, z, rolled), jnp.where(left, rolled, z)], axis=-1)


def _in_proj_kernel(x_ref, g_ref, w_ref, gqa_ref, gka_ref, gqb_ref, gkb_ref,
                    qa_ref, kd_ref, vd_ref, qbt_ref, kb_ref, vbt_ref):
    x = x_ref[...]
    h = (x * lax.rsqrt(jnp.mean(x * x, axis=-1, keepdims=True) + EPS) * g_ref[...]).astype(BF16)

    widths = (SWA_Q_W, SWA_KV_W, SWA_KV_W, DIFF_QK_W, DIFF_QK_W, DIFF_V_W)
    starts = dict(zip(("qa", "ka", "va", "qb", "kb", "vb"), np.cumsum((0,) + widths[:-1])))

    def proj(name, width):
        return _dot(h, w_ref[:, starts[name]:starts[name] + width])

    z_kb = proj("kb", DIFF_QK_W)
    z_qb = proj("qb", DIFF_QK_W)
    kb_ref[...] = _pair_rms_norm(z_kb, gkb_ref[...]).astype(BF16)
    z_qa = proj("qa", SWA_Q_W)
    qbt_ref[...] = _pair_rms_norm(z_qb, gqb_ref[...]).astype(BF16).T
    z_ka = proj("ka", SWA_KV_W)
    qa_ref[...] = _pair_rms_norm(z_qa, gqa_ref[...]).astype(BF16)
    z_vb = proj("vb", DIFF_V_W)
    kd_ref[...] = _dup_halves(_pair_rms_norm(z_ka, gka_ref[...])).astype(BF16)
    z_va = proj("va", SWA_KV_W)
    vbt_ref[...] = z_vb.astype(BF16).T
    vd_ref[...] = _dup_halves(z_va).astype(BF16)


def _in_proj(x2, g_mix, w_in, gqa, gka, gqb, gkb):
    t = x2.shape[0]
    tm = ATTN_TILE
    row = lambda w: pl.BlockSpec((tm, w), lambda i: (i, 0))
    full = lambda a: pl.BlockSpec(a.shape, lambda i: (0, 0), pipeline_mode=pl.Buffered(1))
    tposed = lambda w: pl.BlockSpec((None, w, tm), lambda i: (i, 0, 0))
    bf = lambda w: jax.ShapeDtypeStruct((t, w), BF16)
    bft = lambda w: jax.ShapeDtypeStruct((t // tm, w, tm), BF16)
    return pl.pallas_call(
        _in_proj_kernel,
        grid=(t // tm,),
        in_specs=[row(D_MODEL), full(g_mix), full(w_in), full(gqa), full(gka), full(gqb), full(gkb)],
        out_specs=[row(SWA_Q_W), row(2 * SWA_KV_W), row(2 * SWA_KV_W), tposed(DIFF_QK_W), row(DIFF_QK_W),
                   tposed(DIFF_V_W)],
        out_shape=[bf(SWA_Q_W), bf(2 * SWA_KV_W), bf(2 * SWA_KV_W), bft(DIFF_QK_W), bf(DIFF_QK_W),
                   bft(DIFF_V_W)],
        compiler_params=pltpu.CompilerParams(dimension_semantics=("parallel",),
                                             vmem_limit_bytes=VMEM_LIMIT),
        name="in_proj",
    )(x2, g_mix, w_in, gqa, gka, gqb, gkb)


def _swa_kernel(q_ref, kd_ref, vd_ref, tbl_ref, sink_ref, o_ref):
    i = pl.program_id(1)
    blocks = SWA_TILE // BLOCK

    def scores(n, g, key_rows, table_rows):
        rows = slice(n * BLOCK, (n + 1) * BLOCK)
        stacked = []
        for j in range(2):
            qp = q_ref[rows, (2 * g + j) * LANES:(2 * g + j + 1) * LANES]
            left = _half_lane_mask(qp.shape)
            zero = jnp.zeros_like(qp)
            stacked += [jnp.where(left, qp, zero), jnp.where(left, zero, qp)]
        kk = kd_ref[key_rows, g * LANES:(g + 1) * LANES]
        bias = jnp.concatenate([tbl_ref[SWA_GROUP * g + j, table_rows, :] for j in range(SWA_GROUP)], axis=1)
        return _dot_nt(kk, jnp.concatenate(stacked, axis=0)) + bias

    def finish(n, g, key_rows, s):
        rows = slice(n * BLOCK, (n + 1) * BLOCK)
        sink = sink_ref[g]
        m = jnp.maximum(jnp.max(s, axis=0, keepdims=True), sink)
        p = jnp.exp2(s - m)
        denom = jnp.sum(p, axis=0, keepdims=True) + jnp.exp2(sink - m)
        p = (p * (1.0 / denom)).astype(BF16)
        vv = vd_ref[key_rows, g * LANES:(g + 1) * LANES]
        o = lax.dot_general(p, vv, (((0,), (0,)), ((), ())), preferred_element_type=F32)
        for j in range(2):
            a = o[(2 * j) * BLOCK:(2 * j + 1) * BLOCK]
            b = o[(2 * j + 1) * BLOCK:(2 * j + 2) * BLOCK]
            left = _half_lane_mask(a.shape)
            o_ref[rows, (2 * g + j) * LANES:(2 * g + j + 1) * LANES] = jnp.where(left, a, b).astype(BF16)

    def run(units):
        s_next = scores(*units[0])
        for k, (n, g, key_rows, _) in enumerate(units):
            s = s_next
            if k + 1 < len(units):
                s_next = scores(*units[k + 1])
            finish(n, g, key_rows, s)

    def with_previous(n):
        start = pl.multiple_of(i * SWA_TILE + (n - 1) * BLOCK, BLOCK)
        return [(n, g, pl.ds(start, 2 * BLOCK), slice(0, 2 * BLOCK)) for g in range(SWA_KV_HEADS)]

    later_blocks = [u for n in range(1, blocks) for u in with_previous(n)]

    @pl.when(i == 0)
    def _():
        run([(0, g, slice(0, BLOCK), slice(BLOCK, 2 * BLOCK)) for g in range(SWA_KV_HEADS)] + later_blocks)

    @pl.when(i > 0)
    def _():
        run(with_previous(0) + later_blocks)


def _swa_attention(qa, kd, vd, tbl, sink):
    b, s, _ = qa.shape
    tq = SWA_TILE
    return pl.pallas_call(
        _swa_kernel,
        grid=(b, s // tq),
        in_specs=[pl.BlockSpec((None, tq, SWA_Q_W), lambda bi, i: (bi, i, 0)),
                  pl.BlockSpec((None, s, 2 * SWA_KV_W), lambda bi, i: (bi, 0, 0)),
                  pl.BlockSpec((None, s, 2 * SWA_KV_W), lambda bi, i: (bi, 0, 0)),
                  pl.BlockSpec(tbl.shape, lambda bi, i: (0, 0, 0)),
                  pl.BlockSpec(sink.shape, lambda bi, i: (0, 0, 0))],
        out_specs=pl.BlockSpec((None, tq, SWA_Q_W), lambda bi, i: (bi, i, 0)),
        out_shape=jax.ShapeDtypeStruct((b, s, SWA_Q_W), BF16),
        compiler_params=pltpu.CompilerParams(dimension_semantics=("parallel", "parallel"),
                                             vmem_limit_bytes=VMEM_LIMIT),
        name="swa_attention",
    )(qa, kd, vd, tbl, sink)


def _diff_query_tile(i, lam, qt_ref, k_ref, vt_ref, tbl_ref, gain_ref, o_ref, m_sc, l_sc, acc_sc, s_sc, smax_sc,
                     lam_init):
    tq = ATTN_TILE
    qt = qt_ref[i]
    top = lax.broadcasted_iota(jnp.int32, qt.shape, 0) < HEAD_DIM
    zero = jnp.zeros_like(qt)
    qts = (jnp.where(top, qt, zero), jnp.where(top, zero, qt))

    def write_output(tile):
        o = acc_sc[0] / l_sc[0] - lam * (acc_sc[1] / l_sc[1])
        y = o * lax.rsqrt(jnp.mean(o * o, axis=0, keepdims=True) + EPS) * (gain_ref[...] * (1.0 - lam_init))
        o_ref[pl.ds(pl.multiple_of(tile * tq, tq), tq), :] = y.astype(BF16).T

    def prefetch_scores(j, c):
        kc = k_ref[pl.ds(pl.multiple_of(j * tq, tq), tq), :]
        s = _dot(kc, qts[c])
        s_sc[c] = s
        smax_sc[c] = jnp.max(s, axis=0, keepdims=True)

    def biased_tile(c, kind):
        s = s_sc[c]
        nq = tq // BLOCK
        blk = lambda kb, qb: s[kb * BLOCK:(kb + 1) * BLOCK, qb * BLOCK:(qb + 1) * BLOCK]
        band, diag = tbl_ref[:BLOCK, :], tbl_ref[BLOCK:, :]
        if kind == DIAGONAL:
            return [[blk(kb, qb) + diag if kb == qb else blk(kb, qb) + band if kb == qb - 1 else blk(kb, qb)
                     for kb in range(qb + 1)] for qb in range(nq)]
        return [[blk(kb, qb) + band if (kb, qb) == (nq - 1, 0) else blk(kb, qb) for kb in range(nq)]
                for qb in range(nq)]

    def softmax(c, kind, first):
        if kind == FAR:
            s = s_sc[c]
            m_new = smax_sc[c]
        else:
            cols = biased_tile(c, kind)
            m_new = jnp.concatenate(
                [functools.reduce(jnp.maximum, [jnp.max(b, axis=0, keepdims=True) for b in col]) for col in cols],
                axis=1)
        alpha = None
        if not first:
            m_prev = m_sc[c]
            m_new = jnp.maximum(m_prev, m_new)
            alpha = jnp.exp2(m_prev - m_new)
        if kind == FAR:
            p = jnp.exp2(s - m_new)
            return m_new, alpha, p, jnp.sum(p, axis=0, keepdims=True)
        p_cols, sums = [], []
        for qb, col in enumerate(cols):
            m_col = m_new[:, qb * BLOCK:(qb + 1) * BLOCK]
            ps = [jnp.exp2(b - m_col) for b in col]
            sums.append(sum(jnp.sum(p, axis=0, keepdims=True) for p in ps))
            masked = [jnp.zeros((BLOCK, BLOCK), F32)] * (tq // BLOCK - len(col))
            p_cols.append(jnp.concatenate(ps + masked, axis=0))
        return m_new, alpha, jnp.concatenate(p_cols, axis=1), jnp.concatenate(sums, axis=1)

    def step(j, j_next, kind, first):
        vct = vt_ref[j]
        for c in range(2):
            m_new, alpha, p, psum = softmax(c, kind, first)
            prefetch_scores(j_next, c)
            pv = _dot(vct, p.astype(BF16))
            m_sc[c] = m_new
            if first:
                l_sc[c] = psum
                acc_sc[c] = pv
            else:
                l_sc[c] = alpha * l_sc[c] + psum
                acc_sc[c] = alpha * acc_sc[c] + pv

    def diagonal_step(j_next, finish_previous):
        for c in range(2):
            prefetch_scores(i, c)
        if finish_previous:
            write_output(i - 1)
        step(i, j_next, DIAGONAL, True)

    @pl.when(i == 0)
    def _():
        diagonal_step(0, False)

    @pl.when(i > 0)
    def _():
        diagonal_step(i - 1, True)
        step(i - 1, jnp.maximum(i - 2, 0), PREVIOUS, False)

    n_far = jnp.maximum(i - 1, 0)

    def far_steps(j, count):
        for u in range(count):
            step(j - u, jnp.maximum(j - u - 1, 0), FAR, False)

    done = 0
    size = 1
    while size < FAR_UNROLL:
        @pl.when(n_far & size > 0)
        def _(done=done, size=size):
            far_steps(i - 2 - done, size)

        done = done + (n_far & size)
        size *= 2

    def far_group(t, carry):
        far_steps(i - 2 - done - FAR_UNROLL * t, FAR_UNROLL)
        return carry

    lax.fori_loop(0, lax.shift_right_logical(n_far, FAR_UNROLL.bit_length() - 1), far_group, 0)

    @pl.when(i == qt_ref.shape[0] - 1)
    def _():
        write_output(i)


def _diff_kernel(lq1_ref, lk1_ref, lq2_ref, lk2_ref, qt_ref, k_ref, vt_ref, tbl_ref, gain_ref, o_ref,
                 m_sc, l_sc, acc_sc, s_sc, smax_sc, *, lam_init):
    lam = (jnp.exp(jnp.sum(lq1_ref[...] * lk1_ref[...], axis=-1, keepdims=True))
           - jnp.exp(jnp.sum(lq2_ref[...] * lk2_ref[...], axis=-1, keepdims=True)) + lam_init)

    def query_tile(i, carry):
        _diff_query_tile(i, lam, qt_ref, k_ref, vt_ref, tbl_ref, gain_ref, o_ref, m_sc, l_sc, acc_sc, s_sc,
                         smax_sc, lam_init)
        return carry

    lax.fori_loop(0, qt_ref.shape[0], query_tile, 0)


def _diff_attention(lam_vecs, qbt, kb, vbt, tbl, gain, lam_init):
    b, s, _ = kb.shape
    tq = ATTN_TILE
    n = s // tq
    vec = pl.BlockSpec((1, HEAD_DIM), lambda bi, h: (0, 0))
    return pl.pallas_call(
        functools.partial(_diff_kernel, lam_init=lam_init),
        grid=(b, DIFF_HEADS),
        in_specs=[vec, vec, vec, vec,
                  pl.BlockSpec((None, n, LANES, tq), lambda bi, h: (bi, 0, h, 0)),
                  pl.BlockSpec((None, s, LANES), lambda bi, h: (bi, 0, h)),
                  pl.BlockSpec((None, n, LANES, tq), lambda bi, h: (bi, 0, h, 0)),
                  pl.BlockSpec((None, 2 * BLOCK, BLOCK), lambda bi, h: (h, 0, 0)),
                  pl.BlockSpec((DIFF_V_DIM, 1), lambda bi, h: (0, 0))],
        out_specs=pl.BlockSpec((None, s, LANES), lambda bi, h: (bi, 0, h)),
        out_shape=jax.ShapeDtypeStruct((b, s, DIFF_V_W), BF16),
        scratch_shapes=[pltpu.VMEM((2, 1, tq), F32), pltpu.VMEM((2, 1, tq), F32),
                        pltpu.VMEM((2, DIFF_V_DIM, tq), F32), pltpu.VMEM((2, tq, tq), F32),
                        pltpu.VMEM((2, 1, tq), F32)],
        compiler_params=pltpu.CompilerParams(dimension_semantics=("parallel", "parallel"),
                                             vmem_limit_bytes=VMEM_LIMIT),
        name="diff_attention",
    )(*lam_vecs, qbt, kb, vbt, tbl, gain)


def _merge_kernel(x_ref, ya_ref, yb_ref, g_ref, wg_ref, wa_ref, wb_ref, wo_ref, o_ref):
    x = x_ref[...]
    h = (x * lax.rsqrt(jnp.mean(x * x, axis=-1, keepdims=True) + EPS) * g_ref[...]).astype(BF16)
    ga = jax.nn.sigmoid(_dot(h, wg_ref[:, :D_MODEL]))
    mixed = ga * _dot(ya_ref[...], wa_ref[...])
    gb = jax.nn.sigmoid(_dot(h, wg_ref[:, D_MODEL:]))
    mixed = mixed + gb * _dot(yb_ref[...], wb_ref[...])
    o_ref[...] = x + _dot(mixed.astype(BF16), wo_ref[...])


def _merge(x2, ya, yb, g_mix, w_gate, wa, wb, wo):
    t = x2.shape[0]
    tm = ROW_TILE
    row = lambda w: pl.BlockSpec((tm, w), lambda i: (i, 0))
    full = lambda a: pl.BlockSpec(a.shape, lambda i: (0, 0), pipeline_mode=pl.Buffered(1))
    return pl.pallas_call(
        _merge_kernel,
        grid=(t // tm,),
        in_specs=[row(D_MODEL), row(SWA_Q_W), row(DIFF_V_W), full(g_mix), full(w_gate), full(wa), full(wb),
                  full(wo)],
        out_specs=row(D_MODEL),
        out_shape=jax.ShapeDtypeStruct((t, D_MODEL), F32),
        compiler_params=pltpu.CompilerParams(dimension_semantics=("parallel",),
                                             vmem_limit_bytes=VMEM_LIMIT),
        name="merge",
    )(x2, ya, yb, g_mix, w_gate, wa, wb, wo)


def _conv_mlp_kernel(x_ref, halo_ref, g_ref, wup_ref, cw_ref, cb_ref, wdn_ref, o_ref, u_sc, acc_sc):
    tm = MLP_TILE
    n_chunks = D_FF // FF_CHUNK
    first = pl.program_id(1) == 0

    def normed(x):
        return (x * lax.rsqrt(jnp.mean(x * x, axis=-1, keepdims=True) + EPS) * g_ref[...]).astype(BF16)

    x = x_ref[...]
    h = normed(x)
    h_halo = normed(halo_ref[...])
    keep = jnp.where(first, 0.0, 1.0)

    def columns(c, part):
        return slice(part * D_FF + c * FF_CHUNK, part * D_FF + (c + 1) * FF_CHUNK)

    def up(c):
        for part in range(2):
            cols = columns(c, part)
            u_sc[c % 2, part, 0:SUBLANES, :] = _dot(h_halo, wup_ref[:, cols]) * keep
            u_sc[c % 2, part, SUBLANES:, :] = _dot(h, wup_ref[:, cols])

    def conv(c, part):
        cols = columns(c, part)
        w = cw_ref[:, cols]
        u = u_sc.at[c % 2, part]
        y = (u[SUBLANES - 2:SUBLANES - 2 + tm, :] * w[0:1]
             + u[SUBLANES - 1:SUBLANES - 1 + tm, :] * w[1:2]
             + u[SUBLANES:SUBLANES + tm, :] * w[2:3])
        return y + cb_ref[:, cols]

    up(0)
    for c in range(n_chunks):
        if c + 1 < n_chunks:
            up(c + 1)
        half_g = 0.5 * conv(c, 0)
        act = ((half_g + half_g * jnp.tanh(half_g)) * conv(c, 1)).astype(BF16)
        contrib = _dot(act, wdn_ref[columns(c, 0), :])
        if c + 1 == n_chunks:
            o_ref[...] = x + acc_sc[...] + contrib
        elif c == 0:
            acc_sc[...] = contrib
        else:
            acc_sc[...] += contrib


def _conv_mlp(x3, g_ffn, w_up, conv_w, conv_b, w_down):
    b, s, _ = x3.shape
    tm = MLP_TILE
    halo_blocks = tm // SUBLANES
    full = lambda a: pl.BlockSpec(a.shape, lambda bi, i: (0,) * a.ndim, pipeline_mode=pl.Buffered(1))
    return pl.pallas_call(
        _conv_mlp_kernel,
        grid=(b, s // tm),
        in_specs=[pl.BlockSpec((None, tm, D_MODEL), lambda bi, i: (bi, i, 0)),
                  pl.BlockSpec((None, SUBLANES, D_MODEL),
                               lambda bi, i: (bi, jnp.maximum(i * halo_blocks - 1, 0), 0)),
                  full(g_ffn), full(w_up), full(conv_w), full(conv_b), full(w_down)],
        out_specs=pl.BlockSpec((None, tm, D_MODEL), lambda bi, i: (bi, i, 0)),
        out_shape=jax.ShapeDtypeStruct((b, s, D_MODEL), F32),
        scratch_shapes=[pltpu.VMEM((2, 2, tm + SUBLANES, FF_CHUNK), F32),
                        pltpu.VMEM((tm, D_MODEL), F32)],
        compiler_params=pltpu.CompilerParams(dimension_semantics=("parallel", "parallel"),
                                             vmem_limit_bytes=VMEM_LIMIT),
        name="conv_mlp",
    )(x3, x3, g_ffn, w_up, conv_w, conv_b, w_down)


def _swa_bucket_idx():
    rel = BLOCK + np.arange(BLOCK)[None, :] - np.arange(2 * BLOCK)[:, None]
    return np.where((rel >= 0) & (rel < WINDOW), _t5_bucket_np(rel), -1).astype(np.int32)


def _diff_bucket_idx():
    rel = np.arange(BLOCK)[None, :] - np.arange(BLOCK)[:, None]
    below = _t5_bucket_np(rel + BLOCK)
    diagonal = np.where(rel >= 0, _t5_bucket_np(rel), -1)
    return np.concatenate([below, diagonal]).astype(np.int32)


def _layer(x, l, swa_tbl, diff_tbl, g_mix, w_in, qn_a, kn_a, sinks, qn_b, kn_b, lam_q1, lam_k1,
           lam_q2, lam_k2, subln_b, w_br_a, w_br_b, w_o, g_ffn, w_up, conv_w, conv_b, w_down):
    b, s, _ = x.shape
    t = b * s
    lam_init = 0.8 - 0.6 * math.exp(-0.3 * l)
    q_scale = HEAD_DIM ** -0.5 * LOG2E
    tile = lambda v, reps, scale=1.0: (jnp.tile(v.astype(F32), reps) * scale).reshape(1, -1)

    x2 = x.reshape(t, D_MODEL)
    w_in_bf = w_in[l].astype(BF16)
    g_mix_row = g_mix[l].reshape(1, -1)
    qa, kd, vd, qbt, kb, vbt = _in_proj(
        x2, g_mix_row, w_in_bf[:, :IN_WIDTH - GATE_W],
        tile(qn_a[l], SWA_Q_HEADS, q_scale), tile(kn_a[l], SWA_KV_HEADS),
        tile(qn_b[l], 2 * DIFF_HEADS, q_scale), tile(kn_b[l], 2 * DIFF_HEADS))

    sink = jnp.repeat(sinks[l].astype(F32) * LOG2E, BLOCK).reshape(SWA_KV_HEADS, 1, SWA_GROUP * BLOCK)
    ya = _swa_attention(qa.reshape(b, s, -1), kd.reshape(b, s, -1), vd.reshape(b, s, -1), swa_tbl, sink)

    lam_vecs = [v[l].astype(F32).reshape(1, HEAD_DIM) for v in (lam_q1, lam_k1, lam_q2, lam_k2)]
    chunks = s // ATTN_TILE
    yb = _diff_attention(lam_vecs, qbt.reshape(b, chunks, DIFF_QK_W, ATTN_TILE), kb.reshape(b, s, -1),
                         vbt.reshape(b, chunks, DIFF_V_W, ATTN_TILE), diff_tbl,
                         subln_b[l].astype(F32).reshape(-1, 1), lam_init)

    x2 = _merge(x2, ya.reshape(t, -1), yb.reshape(t, -1), g_mix_row, w_in_bf[:, IN_WIDTH - GATE_W:],
                w_br_a[l].astype(BF16), w_br_b[l].astype(BF16), w_o[l].astype(BF16))
    x3 = _conv_mlp(x2.reshape(b, s, D_MODEL), g_ffn[l].reshape(1, -1), w_up[l].astype(BF16),
                   conv_w[l].astype(F32), conv_b[l].astype(F32).reshape(1, -1), w_down[l].astype(BF16))
    return x3


def kernel(x, rel_bias, g_mix, w_in, qn_a, kn_a, sinks, qn_b, kn_b, lam_q1, lam_k1, lam_q2, lam_k2, subln_b,
           w_br_a, w_br_b, w_o, g_ffn, w_up, conv_w, conv_b, w_down):
    rb = rel_bias.astype(F32)
    swa_tbl = _bias_tables(rb, _swa_bucket_idx(), SWA_Q_HEADS, 0, False)
    diff_tbl = _bias_tables(rb, _diff_bucket_idx(), DIFF_HEADS, SWA_Q_HEADS, True)
    for l in range(g_mix.shape[0]):
        x = _layer(x, l, swa_tbl, diff_tbl, g_mix, w_in, qn_a, kn_a, sinks, qn_b, kn_b, lam_q1, lam_k1,
                   lam_q2, lam_k2, subln_b, w_br_a, w_br_b, w_o, g_ffn, w_up, conv_w, conv_b, w_down)
    return x
```
